```python
import jax, jax.numpy as jnp
from jax import lax
import numpy as np

D_MODEL = 1024
BATCH = 8
SEQ = 4096
DEPTH = 4

GRID_W = 64
HEAD_DIM = 64
EPS = 1e-6
ROPE_THETA = 10000.0
Q_BLOCK = 128
NEG_INF = -1e30
NA_HEADS = 4
NA_WIN_ROWS = 8
NA_WIN_COLS = 16
GQA_Q_HEADS = 6
GQA_KV_HEADS = 2
DIL_GROUPS = ((128, 1), (512, 4), (2048, 16))
DIL_HEADS_PER_GROUP = 2
DIL_HEADS = len(DIL_GROUPS) * DIL_HEADS_PER_GROUP
QA_W = NA_HEADS * HEAD_DIM
QB_W = GQA_Q_HEADS * HEAD_DIM
QC_W = DIL_HEADS * HEAD_DIM
KA_W = NA_HEADS * HEAD_DIM
KB_W = GQA_KV_HEADS * HEAD_DIM
KC_W = DIL_HEADS * HEAD_DIM
Q_W = QA_W + QB_W + QC_W
KV_W = KA_W + KB_W + KC_W
N_BRANCHES = 3
GATE_W = N_BRANCHES * D_MODEL
IN_W = Q_W + 2 * KV_W + GATE_W
BR_A = NA_HEADS * HEAD_DIM
BR_B = GQA_Q_HEADS * HEAD_DIM
BR_C = DIL_HEADS_PER_GROUP * HEAD_DIM
BRANCH_W = BR_A + BR_B + BR_C
FFN_DIM = 2816
N_EXPERTS = 8
TOP_K = 2
EXPERT_DIM = 3584

kernel_name = 'hybrid_gated_mixer_encoder'


def rms_norm(x, g):
    xf = x.astype(jnp.float32)
    y = xf * lax.rsqrt(jnp.mean(xf * xf, axis=-1, keepdims=True) + EPS)
    return (y * g.astype(jnp.float32)).astype(x.dtype)


def rope_angles(pos, dim):
    inv_freq = ROPE_THETA ** (-jnp.arange(0, dim, 2, dtype=jnp.float32) / dim)
    return pos.astype(jnp.float32)[:, None] * inv_freq[None, :]


def apply_rope(x, ang):
    half = x.shape[-1] // 2
    cos = jnp.cos(ang).astype(x.dtype)
    sin = jnp.sin(ang).astype(x.dtype)
    x1, x2 = x[..., :half], x[..., half:]
    return jnp.concatenate([x1 * cos - x2 * sin, x2 * cos + x1 * sin], axis=-1)


def axial_rope(x, ang_row, ang_col):
    half = x.shape[-1] // 2
    return jnp.concatenate([apply_rope(x[..., :half], ang_row), apply_rope(x[..., half:], ang_col)], axis=-1)


def to_heads(t, n):
    b, s, _ = t.shape
    return t.reshape(b, s, n, HEAD_DIM).transpose(0, 2, 1, 3)


def from_heads(t):
    b, h, s, dh = t.shape
    return t.transpose(0, 2, 1, 3).reshape(b, s, h * dh)


def neighbourhood_attention(q, k, v, rpb):
    b, h, s, dh = q.shape
    rows = s // GRID_W
    kh = min(NA_WIN_ROWS, rows)
    kw = NA_WIN_COLS
    scale = HEAD_DIM ** -0.5
    qg = q.reshape(b, h, rows, GRID_W, dh).astype(jnp.float32)
    kg = k.reshape(b, h, rows, GRID_W, dh).astype(jnp.float32)
    vg = v.reshape(b, h, rows, GRID_W, dh)
    r = jnp.arange(rows)
    row_start = jnp.clip(r - kh // 2, 0, rows - kh)
    key_rows = row_start[:, None] + jnp.arange(kh)[None, :]
    c = jnp.arange(GRID_W)
    col_start = jnp.clip(c - kw // 2, 0, GRID_W - kw)
    col_mask = (c[None, :] >= col_start[:, None]) & (c[None, :] < col_start[:, None] + kw)
    k_band = kg[:, :, key_rows]
    v_band = vg[:, :, key_rows]
    sc = jnp.einsum('bhrqd,bhrkcd->bhrqkc', qg, k_band) * scale
    dr_idx = key_rows - r[:, None] + NA_WIN_ROWS - 1
    dc_idx = jnp.clip(c[None, :] - c[:, None], -(kw - 1), kw - 1) + NA_WIN_COLS - 1
    bias = rpb.astype(jnp.float32)[:, dr_idx[:, None, :, None], dc_idx[None, :, None, :]]
    sc = jnp.where(col_mask[:, None, :], sc + bias, NEG_INF)
    p = jax.nn.softmax(sc.reshape(b, h, rows, GRID_W, kh * GRID_W), axis=-1)
    p = p.reshape(b, h, rows, GRID_W, kh, GRID_W).astype(v.dtype)
    o = jnp.einsum('bhrqkc,bhrkcd->bhrqd', p, v_band)
    return o.reshape(b, h, s, dh)


def gqa_attention(q, k, v):
    b, hq, s, dh = q.shape
    hkv = k.shape[1]
    g = hq // hkv
    nb = s // Q_BLOCK
    scale = HEAD_DIM ** -0.5
    kf = k.astype(jnp.float32)
    qb = q.reshape(b, hkv, g, nb, Q_BLOCK, dh).transpose(3, 0, 1, 2, 4, 5)

    def block(qi):
        sc = jnp.einsum('bkgqd,bksd->bkgqs', qi.astype(jnp.float32), kf) * scale
        p = jax.nn.softmax(sc, axis=-1).astype(v.dtype)
        return jnp.einsum('bkgqs,bksd->bkgqd', p, v)

    o = lax.map(block, qb)
    return o.transpose(1, 2, 3, 0, 4, 5).reshape(b, hq, s, dh)


def dilated_attention(q, k, v):
    b, h, s, dh = q.shape
    hpg = DIL_HEADS_PER_GROUP
    nb = s // Q_BLOCK
    scale = HEAD_DIM ** -0.5
    k_groups = [k[:, gi * hpg:(gi + 1) * hpg].astype(jnp.float32) for gi in range(len(DIL_GROUPS))]
    v_groups = [v[:, gi * hpg:(gi + 1) * hpg] for gi in range(len(DIL_GROUPS))]
    qb = q.reshape(b, h, nb, Q_BLOCK, dh).transpose(2, 0, 1, 3, 4)

    def block(args):
        qi, bi = args
        t = bi * Q_BLOCK + jnp.arange(Q_BLOCK)
        outs, lses = [], []
        for gi, (window, dil) in enumerate(DIL_GROUPS):
            side = window // dil // 2
            pos = t[:, None] + dil * jnp.arange(-side, side + 1)[None, :]
            valid = (pos >= 0) & (pos < s)
            pos = jnp.clip(pos, 0, s - 1)
            kgat = k_groups[gi][:, :, pos]
            vgat = v_groups[gi][:, :, pos]
            qg = qi[:, gi * hpg:(gi + 1) * hpg].astype(jnp.float32)
            sc = jnp.einsum('bhqd,bhqnd->bhqn', qg, kgat) * scale
            sc = jnp.where(valid, sc, NEG_INF)
            lse = jax.nn.logsumexp(sc, axis=-1)
            p = jnp.exp(sc - lse[..., None]).astype(v.dtype)
            outs.append(jnp.einsum('bhqn,bhqnd->bhqd', p, vgat).astype(jnp.float32))
            lses.append(lse)
        w = jax.nn.softmax(jnp.stack(lses), axis=0)
        o = jnp.sum(w[..., None] * jnp.stack(outs), axis=0)
        return o.astype(v.dtype)

    o = lax.map(block, (qb, jnp.arange(nb)))
    return o.transpose(1, 2, 0, 3, 4).reshape(b, hpg, s, dh)


def swiglu(h, w_gu, w_down):
    gu = h @ w_gu
    g, u = jnp.split(gu, 2, axis=-1)
    return (jax.nn.silu(g) * u) @ w_down


def moe_swiglu(h, w_router, w_gu, w_down):
    b, s, d = h.shape
    ht = h.reshape(b * s, d)
    logits = (ht @ w_router).astype(jnp.float32)
    top_val, top_idx = lax.top_k(logits, TOP_K)
    top_w = jax.nn.softmax(top_val, axis=-1)
    gates = jnp.sum(jax.nn.one_hot(top_idx, N_EXPERTS, dtype=jnp.float32) * top_w[..., None], axis=1)
    gates = gates.astype(h.dtype)
    out = jnp.zeros_like(ht)
    for e in range(N_EXPERTS):
        out = out + gates[:, e:e + 1] * swiglu(ht, w_gu[e], w_down[e])
    return out.reshape(b, s, d)


def setup_inputs(seed: int = 0) -> dict:
    key = jax.random.key(seed)
    ks = jax.random.split(key, 12)
    n_dense = (DEPTH + 1) // 2
    n_moe = DEPTH // 2
    f32 = jnp.float32
    nrm = jax.random.normal
    x = nrm(ks[0], (BATCH, SEQ, D_MODEL), f32)
    norm_gains = 1.0 + 0.1 * nrm(ks[1], (DEPTH, 2, D_MODEL), f32)
    w_in = nrm(ks[2], (DEPTH, D_MODEL, IN_W), f32) * D_MODEL ** -0.5
    qk_gains = 1.0 + 0.1 * nrm(ks[3], (DEPTH, 3, 2, HEAD_DIM), f32)
    na_rpb = 0.1 * nrm(ks[4], (DEPTH, NA_HEADS, 2 * NA_WIN_ROWS - 1, 2 * NA_WIN_COLS - 1), f32)
    branch_scale = jnp.concatenate([
        jnp.full((BR_A,), BR_A ** -0.5, f32),
        jnp.full((BR_B,), BR_B ** -0.5, f32),
        jnp.full((BR_C,), BR_C ** -0.5, f32)])
    w_branch = nrm(ks[5], (DEPTH, BRANCH_W, D_MODEL), f32) * branch_scale[None, :, None]
    w_out = nrm(ks[6], (DEPTH, D_MODEL, D_MODEL), f32) * D_MODEL ** -0.5
    ffn_w_gu = nrm(ks[7], (n_dense, D_MODEL, 2 * FFN_DIM), f32) * D_MODEL ** -0.5
    ffn_w_down = nrm(ks[8], (n_dense, FFN_DIM, D_MODEL), f32) * FFN_DIM ** -0.5
    moe_router = nrm(ks[9], (n_moe, D_MODEL, N_EXPERTS), f32) * D_MODEL ** -0.5
    moe_w_gu = nrm(ks[10], (n_moe, N_EXPERTS, D_MODEL, 2 * EXPERT_DIM), f32) * D_MODEL ** -0.5
    moe_w_down = nrm(ks[11], (n_moe, N_EXPERTS, EXPERT_DIM, D_MODEL), f32) * EXPERT_DIM ** -0.5
    return {'x': x, 'norm_gains': norm_gains, 'w_in': w_in, 'qk_gains': qk_gains,
            'na_rpb': na_rpb, 'w_branch': w_branch, 'w_out': w_out,
            'ffn_w_gu': ffn_w_gu, 'ffn_w_down': ffn_w_down, 'moe_router': moe_router,
            'moe_w_gu': moe_w_gu, 'moe_w_down': moe_w_down}


def reference(x, norm_gains, w_in, qk_gains, na_rpb, w_branch, w_out,
              ffn_w_gu, ffn_w_down, moe_router, moe_w_gu, moe_w_down):
    b, s, d = x.shape
    pos = jnp.arange(s)
    ang_1d = rope_angles(pos, HEAD_DIM)
    ang_row = rope_angles(pos // GRID_W, HEAD_DIM // 2)
    ang_col = rope_angles(pos % GRID_W, HEAD_DIM // 2)
    for layer in range(DEPTH):
        h = rms_norm(x, norm_gains[layer, 0])
        proj = h @ w_in[layer]
        q_all = proj[..., :Q_W]
        k_all = proj[..., Q_W:Q_W + KV_W]
        v_all = proj[..., Q_W + KV_W:Q_W + 2 * KV_W]
        gate = proj[..., Q_W + 2 * KV_W:]
        qk = qk_gains[layer]
        qa = rms_norm(to_heads(q_all[..., :QA_W], NA_HEADS), qk[0, 0])
        ka = rms_norm(to_heads(k_all[..., :KA_W], NA_HEADS), qk[0, 1])
        va = to_heads(v_all[..., :KA_W], NA_HEADS)
        oa = neighbourhood_attention(qa, ka, va, na_rpb[layer])
        qb = axial_rope(rms_norm(to_heads(q_all[..., QA_W:QA_W + QB_W], GQA_Q_HEADS), qk[1, 0]), ang_row, ang_col)
        kb = axial_rope(rms_norm(to_heads(k_all[..., KA_W:KA_W + KB_W], GQA_KV_HEADS), qk[1, 1]), ang_row, ang_col)
        vb = to_heads(v_all[..., KA_W:KA_W + KB_W], GQA_KV_HEADS)
        ob = gqa_attention(qb, kb, vb)
        qc = apply_rope(rms_norm(to_heads(q_all[..., QA_W + QB_W:], DIL_HEADS), qk[2, 0]), ang_1d)
        kc = apply_rope(rms_norm(to_heads(k_all[..., KA_W + KB_W:], DIL_HEADS), qk[2, 1]), ang_1d)
        vc = to_heads(v_all[..., KA_W + KB_W:], DIL_HEADS)
        oc = dilated_attention(qc, kc, vc)
        wb = w_branch[layer]
        ya = from_heads(oa) @ wb[:BR_A]
        yb = from_heads(ob) @ wb[BR_A:BR_A + BR_B]
        yc = from_heads(oc) @ wb[BR_A + BR_B:]
        gt = jax.nn.sigmoid(gate.astype(jnp.float32)).astype(x.dtype).reshape(b, s, N_BRANCHES, d)
        merged = gt[:, :, 0] * ya + gt[:, :, 1] * yb + gt[:, :, 2] * yc
        x = x + merged @ w_out[layer]
        h2 = rms_norm(x, norm_gains[layer, 1])
        if layer % 2 == 0:
            x = x + swiglu(h2, ffn_w_gu[layer // 2], ffn_w_down[layer // 2])
        else:
            i = layer // 2
            x = x + moe_swiglu(h2, moe_router[i], moe_w_gu[i], moe_w_down[i])
    return x
```

```python
import functools

import numpy as np
import jax
import jax.numpy as jnp
from jax import lax
from jax.experimental import pallas as pl
from jax.experimental.pallas import tpu as pltpu

D_MODEL = 1024
HEAD_DIM = 64
GRID_W = 64
EPS = 1e-6
ROPE_THETA = 10000.0
NEG_INF = -1e30
NA_HEADS = 4
NA_WIN_ROWS = 8
NA_WIN_COLS = 16
GQA_Q_HEADS = 6
GQA_KV_HEADS = 2
DILATIONS = (1, 4, 16)
DIL_SIDE = 64
QA_W, QB_W, QC_W = 256, 384, 384
KA_W, KB_W, KC_W = 256, 128, 384
Q_W = QA_W + QB_W + QC_W
KV_W = KA_W + KB_W + KC_W
QK_W = Q_W + KV_W
QKV_W = Q_W + 2 * KV_W
GATE_W = 3 * D_MODEL
N_EXPERTS = 8
LANES = 128
VMEM_LIMIT = 48 * 1024 * 1024

GQA_HEAD_ORDER = (0, 3, 1, 4, 2, 5)

F32 = jnp.float32
BF16 = jnp.bfloat16


def _nt_dot(a, b):
    return lax.dot_general(a, b, (((1,), (1,)), ((), ())), preferred_element_type=F32)


def _rms(x, g):
    ms = jnp.mean(x * x, axis=-1, keepdims=True)
    return x * lax.rsqrt(ms + EPS) * g


_QK_CHUNKS = (
    (0, None, ("qa", 0)), (0, None, ("qa", 1)),
    (2, "axial", ("qb", 0)), (2, "axial", ("qb", 1)), (2, "axial", ("qb", 2)),
    (4, "seq", ("qc", 0)), (4, "seq", ("qc", 1)), (4, "seq", ("qc", 2)),
    (1, None, ("ka", 0)), (1, None, ("ka", 1)),
    (3, "axial", ("kb", 0)),
    (5, "seq", ("kc", 0)), (5, "seq", ("kc", 1)), (5, "seq", ("kc", 2)),
)


def _qkv_kernel(x_ref, g_ref, w_ref, gains_ref, bd_ref, cax_ref, sax_ref, cseq_ref, sseq_ref,
                qa_ref, ka_ref, va_ref, qb_ref, kb_ref, vb_ref, qc_ref, kc_ref, vc_ref):
    tm = x_ref.shape[0]
    h = _rms(x_ref[...], g_ref[...]).astype(BF16)
    p = jnp.dot(h, w_ref[...], preferred_element_type=F32)
    lane = lax.broadcasted_iota(jnp.int32, (tm, LANES), 1)
    dests = {"qa": qa_ref, "ka": ka_ref, "qb": qb_ref, "kb": kb_ref, "qc": qc_ref, "kc": kc_ref}
    sumsq = None
    for c, (grow, rope, (dname, dj)) in enumerate(_QK_CHUNKS):
        if c % 2 == 0:
            pc2 = p[:, c * LANES:(c + 2) * LANES]
            sumsq = jnp.dot((pc2 * pc2).astype(BF16), bd_ref[...], preferred_element_type=F32)
        pc = p[:, c * LANES:(c + 1) * LANES]
        sc = sumsq[:, (c % 2) * LANES:(c % 2 + 1) * LANES]
        y = pc * lax.rsqrt(sc * (1.0 / HEAD_DIM) + EPS) * gains_ref[grow:grow + 1, :]
        if rope is not None:
            half = 16 if rope == "axial" else 32
            cos_t = cax_ref[...] if rope == "axial" else cseq_ref[...]
            sin_t = sax_ref[...] if rope == "axial" else sseq_ref[...]
            partner = jnp.where((lane % (2 * half)) < half,
                                pltpu.roll(y, LANES - half, axis=1), pltpu.roll(y, half, axis=1))
            y = y * cos_t + partner * sin_t
        yb = y.astype(BF16)
        d = dests[dname]
        if dname in ("qc", "kc"):
            d[dj] = yb
        else:
            d[:, dj * LANES:(dj + 1) * LANES] = yb
    v = p[:, QK_W:].astype(BF16)
    va_ref[...] = v[:, :KA_W]
    vb_ref[...] = v[:, KA_W:KA_W + KB_W]
    for j in range(3):
        vc_ref[j] = v[:, KA_W + KB_W + j * LANES:KA_W + KB_W + (j + 1) * LANES]


def _qkv_proj(x, g, w, gains, bd, tabs, seq_len, tm=512):
    n = x.shape[0]
    nb = seq_len // tm
    row = lambda i: (i, 0)
    const = lambda i: (0, 0)
    tab = lambda i: (i % nb, 0)
    out_shapes = [jax.ShapeDtypeStruct((n, w_), BF16) for w_ in (QA_W, KA_W, KA_W, QB_W, KB_W, KB_W)]
    out_shapes += [jax.ShapeDtypeStruct((3, n, LANES), BF16)] * 3
    out_specs = [pl.BlockSpec((tm, w_), row) for w_ in (QA_W, KA_W, KA_W, QB_W, KB_W, KB_W)]
    out_specs += [pl.BlockSpec((3, tm, LANES), lambda i: (0, i, 0))] * 3
    return pl.pallas_call(
        _qkv_kernel,
        grid=(n // tm,),
        in_specs=[pl.BlockSpec((tm, D_MODEL), row), pl.BlockSpec((1, D_MODEL), const),
                  pl.BlockSpec((D_MODEL, QKV_W), const), pl.BlockSpec((8, LANES), const),
                  pl.BlockSpec((2 * LANES, 2 * LANES), const)] + [pl.BlockSpec((tm, LANES), tab)] * 4,
        out_specs=out_specs,
        out_shape=out_shapes,
        compiler_params=pltpu.CompilerParams(dimension_semantics=("parallel",), vmem_limit_bytes=VMEM_LIMIT),
        name="qkv_proj",
    )(x, g, w, gains, bd, *tabs)


def _na_kernel(q_ref, k_ref, v_ref, bias_ref, o_ref, *, rows_per_step):
    rg = pl.program_id(1)
    n_rows = k_ref.shape[0] // GRID_W
    nk = NA_WIN_ROWS * GRID_W
    lo = lax.broadcasted_iota(jnp.int32, (GRID_W, LANES), 1) < HEAD_DIM

    def body(rr, carry):
        r = rg * rows_per_step + rr
        row_start = jnp.clip(r - NA_WIN_ROWS // 2, 0, n_rows - NA_WIN_ROWS)
        off = r - row_start
        ks = pl.multiple_of(row_start * GRID_W, GRID_W)
        qs = pl.multiple_of(rr * GRID_W, GRID_W)
        qrow = q_ref[pl.ds(qs, GRID_W), :]
        outs = []
        for p in range(NA_HEADS // 2):
            q2 = qrow[:, p * LANES:(p + 1) * LANES]
            z = jnp.zeros_like(q2)
            lhs = jnp.concatenate([jnp.where(lo, q2, z), jnp.where(lo, z, q2)], axis=0)
            kblk = k_ref[pl.ds(ks, nk), p * LANES:(p + 1) * LANES]
            vblk = v_ref[pl.ds(ks, nk), p * LANES:(p + 1) * LANES]
            s = _nt_dot(lhs, kblk)
            s = s + jnp.concatenate([bias_ref[2 * p, off], bias_ref[2 * p + 1, off]], axis=0)
            m = jnp.max(s, axis=-1, keepdims=True)
            e = jnp.exp(s - m)
            l = jnp.sum(e, axis=-1, keepdims=True)
            pv = jnp.dot(e.astype(BF16), vblk, preferred_element_type=F32) / l
            outs.append(jnp.where(lo, pv[:GRID_W], pv[GRID_W:]))
        o_ref[pl.ds(qs, GRID_W), :] = jnp.concatenate(outs, axis=1).astype(o_ref.dtype)
        return carry

    lax.fori_loop(0, rows_per_step, body, 0)


def _na_bias_table(rpb):
    kw = NA_WIN_COLS
    c = np.arange(GRID_W)
    col_start = np.clip(c - kw // 2, 0, GRID_W - kw)
    col_mask = (c[None, :] >= col_start[:, None]) & (c[None, :] < col_start[:, None] + kw)
    dc = np.clip(c[None, :] - c[:, None], -(kw - 1), kw - 1) + NA_WIN_COLS - 1
    off = np.arange(NA_WIN_ROWS)
    kr = np.arange(NA_WIN_ROWS)
    dr = kr[None, :] - off[:, None] + NA_WIN_ROWS - 1
    b = rpb.astype(F32)[:, dr[:, None, :, None], dc[None, :, None, :]]
    b = jnp.where(col_mask[None, None, :, None, :], b, NEG_INF)
    return b.reshape(rpb.shape[0], NA_WIN_ROWS, GRID_W, NA_WIN_ROWS * GRID_W)


def _na_attention(q, k, v, bias, batch, seq_len, rows_per_step=8):
    q3, k3, v3 = (t.reshape(batch, seq_len, QA_W) for t in (q, k, v))
    tq = rows_per_step * GRID_W
    out = pl.pallas_call(
        functools.partial(_na_kernel, rows_per_step=rows_per_step),
        grid=(batch, seq_len // tq),
        in_specs=[pl.BlockSpec((None, tq, QA_W), lambda b, i: (b, i, 0)),
                  pl.BlockSpec((None, seq_len, QA_W), lambda b, i: (b, 0, 0)),
                  pl.BlockSpec((None, seq_len, QA_W), lambda b, i: (b, 0, 0)),
                  pl.BlockSpec(bias.shape, lambda b, i: (0, 0, 0, 0))],
        out_specs=pl.BlockSpec((None, tq, QA_W), lambda b, i: (b, i, 0)),
        out_shape=jax.ShapeDtypeStruct((batch, seq_len, QA_W), BF16),
        compiler_params=pltpu.CompilerParams(dimension_semantics=("parallel", "parallel"),
                                             vmem_limit_bytes=VMEM_LIMIT),
        name="na_attn",
    )(q3, k3, v3, bias)
    return out.reshape(batch * seq_len, QA_W)


def _gqa_kernel(q_ref, k_ref, v_ref, o_ref, *, tk):
    tq = q_ref.shape[0]
    n_pairs = QB_W // LANES
    lo = lax.broadcasted_iota(jnp.int32, (tq, LANES), 1) < HEAD_DIM
    q = q_ref[...]
    z = jnp.zeros((tq, LANES), q.dtype)
    pairs = [q[:, p * LANES:(p + 1) * LANES] for p in range(n_pairs)]
    lhs = jnp.concatenate([jnp.where(lo, t, z) for t in pairs] + [jnp.where(lo, z, t) for t in pairs], axis=0)
    rows = lhs.shape[0]

    def body(kc, carry):
        m, l, acc = carry
        ks = pl.multiple_of(kc * tk, tk)
        kblk = k_ref[pl.ds(ks, tk), :]
        vblk = v_ref[pl.ds(ks, tk), :]
        s = _nt_dot(lhs, kblk)
        m_new = jnp.maximum(m, jnp.max(s, axis=-1, keepdims=True))
        e = jnp.exp(s - m_new)
        alpha = jnp.exp(m - m_new)
        l = alpha * l + jnp.sum(e, axis=-1, keepdims=True)
        acc = alpha * acc + jnp.dot(e.astype(BF16), vblk, preferred_element_type=F32)
        return m_new, l, acc

    init = (jnp.full((rows, 1), NEG_INF, F32), jnp.zeros((rows, 1), F32), jnp.zeros((rows, LANES), F32))
    _, l, acc = lax.fori_loop(0, k_ref.shape[0] // tk, body, init)
    o = acc / l
    half = n_pairs * tq
    for p in range(n_pairs):
        o_ref[:, p * LANES:(p + 1) * LANES] = jnp.where(
            lo, o[p * tq:(p + 1) * tq], o[half + p * tq:half + (p + 1) * tq]).astype(o_ref.dtype)


def _gqa_attention(q, k, v, batch, seq_len, tq=128, tk=512):
    q3 = q.reshape(batch, seq_len, QB_W)
    k3 = k.reshape(batch, seq_len, KB_W)
    v3 = v.reshape(batch, seq_len, KB_W)
    out = pl.pallas_call(
        functools.partial(_gqa_kernel, tk=tk),
        grid=(batch, seq_len // tq),
        in_specs=[pl.BlockSpec((None, tq, QB_W), lambda b, i: (b, i, 0)),
                  pl.BlockSpec((None, seq_len, KB_W), lambda b, i: (b, 0, 0)),
                  pl.BlockSpec((None, seq_len, KB_W), lambda b, i: (b, 0, 0))],
        out_specs=pl.BlockSpec((None, tq, QB_W), lambda b, i: (b, i, 0)),
        out_shape=jax.ShapeDtypeStruct((batch, seq_len, QB_W), BF16),
        compiler_params=pltpu.CompilerParams(dimension_semantics=("parallel", "parallel"),
                                             vmem_limit_bytes=VMEM_LIMIT),
        name="gqa_attn",
    )(q3, k3, v3)
    return out.reshape(batch * seq_len, QB_W)


def _dil_kernel(q_ref, k_ref, v_ref, o_ref, lse_ref):
    sub_len = q_ref.shape[0]
    tq = 2 * DIL_SIDE
    tkk = 4 * DIL_SIDE
    lo = lax.broadcasted_iota(jnp.int32, (tq, LANES), 1) < HEAD_DIM
    qi_local = lax.broadcasted_iota(jnp.int32, (2 * tq, tkk), 0) % tq
    ki_local = lax.broadcasted_iota(jnp.int32, (2 * tq, tkk), 1)

    def body(qb, carry):
        i0 = pl.multiple_of(qb * tq, tq)
        ks = pl.multiple_of(jnp.clip(i0 - DIL_SIDE, 0, sub_len - tkk), DIL_SIDE)
        q2 = q_ref[pl.ds(i0, tq), :]
        z = jnp.zeros_like(q2)
        lhs = jnp.concatenate([jnp.where(lo, q2, z), jnp.where(lo, z, q2)], axis=0)
        s = _nt_dot(lhs, k_ref[pl.ds(ks, tkk), :])
        delta = (qi_local + i0) - (ki_local + ks)
        s = jnp.where((delta <= DIL_SIDE) & (delta >= -DIL_SIDE), s, NEG_INF)
        m = jnp.max(s, axis=-1, keepdims=True)
        e = jnp.exp(s - m)
        l = jnp.sum(e, axis=-1, keepdims=True)
        pv = jnp.dot(e.astype(BF16), v_ref[pl.ds(ks, tkk), :], preferred_element_type=F32) / l
        lse = m + jnp.log(l)
        o_ref[pl.ds(i0, tq), :] = jnp.where(lo, pv[:tq], pv[tq:])
        lse_ref[pl.ds(i0, tq), :] = jnp.where(lo, lse[:tq], lse[tq:])
        return carry

    lax.fori_loop(0, sub_len // tq, body, 0)


def _dil_attention(q, k, v, dil, batch, seq_len):
    sub_len = seq_len // dil
    view = lambda t: t.reshape(batch, sub_len, dil * LANES)
    spec = pl.BlockSpec((None, sub_len, LANES), lambda b, r: (b, 0, r))
    o, lse = pl.pallas_call(
        _dil_kernel,
        grid=(batch, dil),
        in_specs=[spec, spec, spec],
        out_specs=[spec, spec],
        out_shape=[jax.ShapeDtypeStruct((batch, sub_len, dil * LANES), F32)] * 2,
        compiler_params=pltpu.CompilerParams(dimension_semantics=("parallel", "parallel"),
                                             vmem_limit_bytes=VMEM_LIMIT),
        name=f"dil_attn_d{dil}",
    )(view(q), view(k), view(v))
    n = batch * seq_len
    return o.reshape(n, LANES), lse.reshape(n, LANES)


def _merge_body(x_ref, g_ref, oa_ref, ob_ref, oc0_ref, oc1_ref, oc2_ref, lse0_ref, lse1_ref, lse2_ref,
                wg_ref, wba_ref, wbb_ref, wbc_ref, wo_ref):
    x = x_ref[...]
    h = _rms(x, g_ref[...]).astype(BF16)
    lse = [lse0_ref[...], lse1_ref[...], lse2_ref[...]]
    mx = jnp.maximum(jnp.maximum(lse[0], lse[1]), lse[2])
    ex = [jnp.exp(t - mx) for t in lse]
    den = ex[0] + ex[1] + ex[2]
    oc = (ex[0] / den) * oc0_ref[...] + (ex[1] / den) * oc1_ref[...] + (ex[2] / den) * oc2_ref[...]
    ya = jnp.dot(oa_ref[...], wba_ref[...], preferred_element_type=F32)
    yb = jnp.dot(ob_ref[...], wbb_ref[...], preferred_element_type=F32)
    yc = jnp.dot(oc.astype(BF16), wbc_ref[...], preferred_element_type=F32)
    merged = None
    for j, y in enumerate((ya, yb, yc)):
        gate = jnp.dot(h, wg_ref[:, j * D_MODEL:(j + 1) * D_MODEL], preferred_element_type=F32)
        t = jax.nn.sigmoid(gate) * y
        merged = t if merged is None else merged + t
    return x + jnp.dot(merged.astype(BF16), wo_ref[...], preferred_element_type=F32)


def _merge_kernel(*refs):
    *ins, out_ref = refs
    out_ref[...] = _merge_body(*ins)


def _merge_route_kernel(*refs):
    *ins, g2_ref, wr_ref, out_ref, ri_ref, rw_ref, cnt_ref, carry_ref = refs
    xn = _merge_body(*ins)
    out_ref[...] = xn
    tm = xn.shape[0]

    @pl.when(pl.program_id(0) == 0)
    def _():
        carry_ref[...] = jnp.zeros_like(carry_ref)

    h2 = _rms(xn, g2_ref[...])
    logits = jnp.dot(h2, wr_ref[...], preferred_element_type=F32, precision=lax.Precision.HIGHEST)
    lane = lax.broadcasted_iota(jnp.int32, (tm, LANES), 1)
    lane_f = lane.astype(F32)
    logits = jnp.where(lane < N_EXPERTS, logits, -jnp.inf)
    v1 = jnp.max(logits, axis=-1, keepdims=True)
    i1 = jnp.min(jnp.where(logits == v1, lane_f, float(LANES)), axis=-1, keepdims=True)
    hot1 = lane_f == i1
    rest = jnp.where(hot1, -jnp.inf, logits)
    v2 = jnp.max(rest, axis=-1, keepdims=True)
    i2 = jnp.min(jnp.where(rest == v2, lane_f, float(LANES)), axis=-1, keepdims=True)
    hot2 = lane_f == i2
    e2 = jnp.exp(v2 - v1)
    w1 = 1.0 / (1.0 + e2)
    w2 = e2 / (1.0 + e2)
    hot = jnp.where(hot1 | hot2, 1.0, 0.0).astype(BF16)
    tri = (lax.broadcasted_iota(jnp.int32, (tm, tm), 0) > lax.broadcasted_iota(jnp.int32, (tm, tm), 1))
    before = jnp.dot(jnp.where(tri, 1.0, 0.0).astype(BF16), hot, preferred_element_type=F32) + carry_ref[0:1, :]
    r1 = jnp.sum(jnp.where(hot1, before, 0.0), axis=-1, keepdims=True)
    r2 = jnp.sum(jnp.where(hot2, before, 0.0), axis=-1, keepdims=True)
    total = carry_ref[0:1, :] + jnp.sum(hot.astype(F32), axis=0, keepdims=True)
    carry_ref[...] = jnp.broadcast_to(total, carry_ref.shape)
    cnt_ref[...] = jnp.broadcast_to(total, cnt_ref.shape)
    ri = jnp.where(lane == 0, i1, jnp.where(lane == 1, i2, jnp.where(lane == 2, r1, r2)))
    ri_ref[...] = ri.astype(jnp.int32)
    rw_ref[...] = jnp.where(lane == 0, w1, jnp.where(lane == 1, w2, 0.0))


def _merge(x, g1, oa, ob, oc, lse, wg, wba, wbb, wbc, wo, router=None, tm=512):
    n = x.shape[0]
    row = lambda i: (i, 0)
    const = lambda i: (0, 0)
    in_specs = [pl.BlockSpec((tm, D_MODEL), row), pl.BlockSpec((1, D_MODEL), const),
                pl.BlockSpec((tm, QA_W), row), pl.BlockSpec((tm, QB_W), row)]
    in_specs += [pl.BlockSpec((tm, LANES), row)] * 6
    in_specs += [pl.BlockSpec((D_MODEL, GATE_W), const), pl.BlockSpec((QA_W, D_MODEL), const),
                 pl.BlockSpec((QB_W, D_MODEL), const), pl.BlockSpec((LANES, D_MODEL), const),
                 pl.BlockSpec((D_MODEL, D_MODEL), const)]
    args = [x, g1, oa, ob, *oc, *lse, wg, wba, wbb, wbc, wo]
    x_shape = jax.ShapeDtypeStruct((n, D_MODEL), F32)
    x_spec = pl.BlockSpec((tm, D_MODEL), row)
    if router is None:
        return pl.pallas_call(
            _merge_kernel, grid=(n // tm,), in_specs=in_specs, out_specs=x_spec, out_shape=x_shape,
            compiler_params=pltpu.CompilerParams(dimension_semantics=("parallel",), vmem_limit_bytes=VMEM_LIMIT),
            name="merge",
        )(*args)
    g2, wr = router
    in_specs += [pl.BlockSpec((1, D_MODEL), const), pl.BlockSpec((D_MODEL, LANES), const)]
    return pl.pallas_call(
        _merge_route_kernel, grid=(n // tm,), in_specs=in_specs,
        out_specs=[x_spec, pl.BlockSpec((tm, LANES), row), pl.BlockSpec((tm, LANES), row),
                   pl.BlockSpec((8, LANES), const)],
        out_shape=[x_shape, jax.ShapeDtypeStruct((n, LANES), jnp.int32), jax.ShapeDtypeStruct((n, LANES), F32),
                   jax.ShapeDtypeStruct((8, LANES), F32)],
        scratch_shapes=[pltpu.VMEM((8, LANES), F32)],
        compiler_params=pltpu.CompilerParams(dimension_semantics=("arbitrary",), vmem_limit_bytes=VMEM_LIMIT),
        name="merge_route",
    )(*args, g2, wr)


def _ffn_kernel(te_ref, tv_ref, x_ref, g_ref, wg_ref, wu_ref, wd_ref, o_ref, h_ref, acc_ref, *, residual):
    i = pl.program_id(0)
    f = pl.program_id(1)

    @pl.when(f == 0)
    def _():
        h_ref[...] = _rms(x_ref[...], g_ref[...]).astype(BF16)
        acc_ref[...] = jnp.zeros_like(acc_ref)

    @pl.when(tv_ref[i] == 1)
    def _():
        h = h_ref[...]
        gg = jnp.dot(h, wg_ref[...], preferred_element_type=F32)
        uu = jnp.dot(h, wu_ref[...], preferred_element_type=F32)
        a = (jax.nn.silu(gg) * uu).astype(BF16)
        acc_ref[...] += jnp.dot(a, wd_ref[...], preferred_element_type=F32)

    @pl.when(f == pl.num_programs(1) - 1)
    def _():
        o_ref[...] = (x_ref[...] + acc_ref[...]) if residual else acc_ref[...]


def _ffn(x, g, w_gu, w_down, tile_expert, tile_valid, *, residual, tm, tf):
    rows = x.shape[0]
    f_dim = w_down.shape[1]
    nf = f_dim // tf
    fe = lambda f, tv, i: f * tv[i] + (nf - 1) * (1 - tv[i])
    grid_spec = pltpu.PrefetchScalarGridSpec(
        num_scalar_prefetch=2,
        grid=(rows // tm, nf),
        in_specs=[pl.BlockSpec((tm, D_MODEL), lambda i, f, te, tv: (i, 0)),
                  pl.BlockSpec((1, D_MODEL), lambda i, f, te, tv: (0, 0)),
                  pl.BlockSpec((None, D_MODEL, tf), lambda i, f, te, tv: (te[i], 0, fe(f, tv, i))),
                  pl.BlockSpec((None, D_MODEL, tf), lambda i, f, te, tv: (te[i], 0, nf + fe(f, tv, i))),
                  pl.BlockSpec((None, tf, D_MODEL), lambda i, f, te, tv: (te[i], fe(f, tv, i), 0))],
        out_specs=pl.BlockSpec((tm, D_MODEL), lambda i, f, te, tv: (i, 0)),
        scratch_shapes=[pltpu.VMEM((tm, D_MODEL), BF16), pltpu.VMEM((tm, D_MODEL), F32)],
    )
    return pl.pallas_call(
        functools.partial(_ffn_kernel, residual=residual),
        grid_spec=grid_spec,
        out_shape=jax.ShapeDtypeStruct((rows, D_MODEL), F32),
        compiler_params=pltpu.CompilerParams(dimension_semantics=("parallel", "arbitrary"),
                                             vmem_limit_bytes=VMEM_LIMIT),
        name="ffn_res" if residual else "ffn_moe",
    )(tile_expert, tile_valid, x, g, w_gu, w_gu, w_down)


def _dispatch_kernel(pos_ref, x_ref, zeros_ref, xs_ref, sem):
    del zeros_ref
    td = x_ref.shape[0]

    def row_copy(t, slot):
        return pltpu.make_async_copy(x_ref.at[pl.ds(t, 1)], xs_ref.at[pl.ds(pos_ref[0, 2 * t + slot], 1)], sem)

    def start(t, carry):
        row_copy(t, 0).start()
        row_copy(t, 1).start()
        return carry

    def wait(t, carry):
        row_copy(t, 0).wait()
        row_copy(t, 1).wait()
        return carry

    lax.fori_loop(0, td, start, 0)
    lax.fori_loop(0, td, wait, 0)


def _dispatch(x, pos, rows_out, td=256):
    n = x.shape[0]
    pos3 = pos.reshape(n // td, 1, 2 * td)
    zeros = jnp.zeros((rows_out, D_MODEL), F32)
    return pl.pallas_call(
        _dispatch_kernel,
        grid=(n // td,),
        in_specs=[pl.BlockSpec((None, 1, 2 * td), lambda i: (i, 0, 0), memory_space=pltpu.SMEM),
                  pl.BlockSpec((td, D_MODEL), lambda i: (i, 0)),
                  pl.BlockSpec(memory_space=pl.ANY)],
        out_specs=pl.BlockSpec(memory_space=pl.ANY),
        out_shape=jax.ShapeDtypeStruct((rows_out, D_MODEL), F32),
        scratch_shapes=[pltpu.SemaphoreType.DMA(())],
        input_output_aliases={2: 0},
        compiler_params=pltpu.CompilerParams(dimension_semantics=("arbitrary",), vmem_limit_bytes=VMEM_LIMIT),
        name="moe_dispatch",
    )(pos3, x, zeros)


def _combine_kernel(pos_ref, x_ref, w_ref, ys_ref, o_ref, buf_ref, sem):
    tc = x_ref.shape[0]

    def row_copy(t, slot):
        return pltpu.make_async_copy(ys_ref.at[pl.ds(pos_ref[0, 2 * t + slot], 1)],
                                     buf_ref.at[slot, pl.ds(t, 1)], sem)

    def start(t, carry):
        row_copy(t, 0).start()
        row_copy(t, 1).start()
        return carry

    def wait(t, carry):
        row_copy(t, 0).wait()
        row_copy(t, 1).wait()
        return carry

    lax.fori_loop(0, tc, start, 0)
    lax.fori_loop(0, tc, wait, 0)
    w = w_ref[...]
    o_ref[...] = x_ref[...] + (w[:, 0:1] * buf_ref[0] + w[:, 1:2] * buf_ref[1])


def _combine(x, w, ys, pos, tc=256):
    n = x.shape[0]
    pos3 = pos.reshape(n // tc, 1, 2 * tc)
    return pl.pallas_call(
        _combine_kernel,
        grid=(n // tc,),
        in_specs=[pl.BlockSpec((None, 1, 2 * tc), lambda i: (i, 0, 0), memory_space=pltpu.SMEM),
                  pl.BlockSpec((tc, D_MODEL), lambda i: (i, 0)),
                  pl.BlockSpec((tc, LANES), lambda i: (i, 0)),
                  pl.BlockSpec(memory_space=pl.ANY)],
        out_specs=pl.BlockSpec((tc, D_MODEL), lambda i: (i, 0)),
        out_shape=jax.ShapeDtypeStruct((n, D_MODEL), F32),
        scratch_shapes=[pltpu.VMEM((2, tc, D_MODEL), F32), pltpu.SemaphoreType.DMA(())],
        compiler_params=pltpu.CompilerParams(dimension_semantics=("arbitrary",), vmem_limit_bytes=VMEM_LIMIT),
        name="moe_combine",
    )(pos3, x, w, ys)


def _moe(x, g2, route_i, route_w, counts, w_gu, w_down, tm=1024, tf=512):
    n = x.shape[0]
    n_tiles = (2 * n) // tm + N_EXPERTS
    cnt = counts[0, :N_EXPERTS].astype(jnp.int32)
    padded = ((cnt + tm - 1) // tm) * tm
    off_end = jnp.cumsum(padded)
    off_start = off_end - padded
    experts = route_i[:, 0:2]
    pos = off_start[experts] + route_i[:, 2:4]
    tile_start = jnp.arange(n_tiles, dtype=jnp.int32) * tm
    tile_valid = (tile_start < off_end[-1]).astype(jnp.int32)
    tile_expert = jnp.minimum(jnp.sum(tile_start[:, None] >= off_end[None, :], axis=1), N_EXPERTS - 1)
    last_expert = tile_expert[jnp.maximum(off_end[-1] // tm - 1, 0)]
    tile_expert = jnp.where(tile_valid == 1, tile_expert, last_expert).astype(jnp.int32)
    xs = _dispatch(x, pos, n_tiles * tm)
    ys = _ffn(xs, g2, w_gu, w_down, tile_expert, tile_valid, residual=False, tm=tm, tf=tf)
    return _combine(x, route_w, ys, pos)


def _rope_tables(seq_len):
    def angles(pos, dim):
        inv_freq = ROPE_THETA ** (-jnp.arange(0, dim, 2, dtype=F32) / dim)
        return pos.astype(F32)[:, None] * inv_freq[None, :]

    pos = jnp.arange(seq_len)
    a_seq = angles(pos, HEAD_DIM)
    a_row = angles(pos // GRID_W, HEAD_DIM // 2)
    a_col = angles(pos % GRID_W, HEAD_DIM // 2)
    cos_seq = jnp.concatenate([jnp.cos(a_seq)] * 2, axis=-1)
    sin_seq = jnp.concatenate([-jnp.sin(a_seq), jnp.sin(a_seq)], axis=-1)
    cos_ax = jnp.concatenate([jnp.cos(a_row)] * 2 + [jnp.cos(a_col)] * 2, axis=-1)
    sin_ax = jnp.concatenate([-jnp.sin(a_row), jnp.sin(a_row), -jnp.sin(a_col), jnp.sin(a_col)], axis=-1)
    two = lambda t: jnp.concatenate([t, t], axis=-1)
    return two(cos_ax), two(sin_ax), two(cos_seq), two(sin_seq)


def _qkv_column_order():
    cols = np.arange(QKV_W)
    qb = QA_W + np.concatenate([np.arange(h * HEAD_DIM, (h + 1) * HEAD_DIM) for h in GQA_HEAD_ORDER])
    cols[QA_W:QA_W + QB_W] = qb
    return cols


def _branch_row_order():
    rows = np.arange(QA_W + QB_W + LANES)
    rows[QA_W:QA_W + QB_W] = QA_W + np.concatenate(
        [np.arange(h * HEAD_DIM, (h + 1) * HEAD_DIM) for h in GQA_HEAD_ORDER])
    return rows


def kernel(x, norm_gains, w_in, qk_gains, na_rpb, w_branch, w_out, ffn_w_gu, ffn_w_down,
           moe_router, moe_w_gu, moe_w_down):
    batch, seq_len, d = x.shape
    depth = w_in.shape[0]
    n = batch * seq_len
    assert d == D_MODEL and seq_len == GRID_W * GRID_W

    tabs = _rope_tables(seq_len)
    w_qkv = w_in[:, :, _qkv_column_order()].astype(BF16)
    w_gate = w_in[:, :, QKV_W:].astype(BF16)
    w_br = w_branch[:, _branch_row_order(), :].astype(BF16)
    w_o = w_out.astype(BF16)
    ffn_gu = ffn_w_gu.astype(BF16)
    ffn_dn = ffn_w_down.astype(BF16)
    moe_gu = moe_w_gu.astype(BF16)
    moe_dn = moe_w_down.astype(BF16)
    router = jnp.pad(moe_router, ((0, 0), (0, 0), (0, LANES - N_EXPERTS)))
    gains = jnp.tile(qk_gains.reshape(depth, 6, HEAD_DIM), (1, 1, 2))
    gains = gains * jnp.where(jnp.arange(6) % 2 == 0, HEAD_DIM ** -0.5, 1.0)[None, :, None]
    gains = jnp.pad(gains, ((0, 0), (0, 2), (0, 0)))
    hd_idx = np.arange(2 * LANES) // HEAD_DIM
    block_ones = jnp.asarray(hd_idx[:, None] == hd_idx[None, :], BF16)
    one_tile = jnp.zeros((n // 1024,), jnp.int32)
    all_valid = jnp.ones((n // 1024,), jnp.int32)

    xf = x.reshape(n, d)
    for layer in range(depth):
        g1 = norm_gains[layer, 0][None, :]
        g2 = norm_gains[layer, 1][None, :]
        qa, ka, va, qb, kb, vb, qc, kc, vc = _qkv_proj(xf, g1, w_qkv[layer], gains[layer], block_ones, tabs, seq_len)
        oa = _na_attention(qa, ka, va, _na_bias_table(na_rpb[layer]), batch, seq_len)
        ob = _gqa_attention(qb, kb, vb, batch, seq_len)
        dil = [_dil_attention(qc[j], kc[j], vc[j], DILATIONS[j], batch, seq_len) for j in range(3)]
        oc = [t[0] for t in dil]
        lse = [t[1] for t in dil]
        wb = w_br[layer]
        margs = (xf, g1, oa, ob, oc, lse, w_gate[layer], wb[:QA_W], wb[QA_W:QA_W + QB_W], wb[QA_W + QB_W:], w_o[layer])
        if layer % 2 == 0:
            xf = _merge(*margs)
            xf = _ffn(xf, g2, ffn_gu[layer // 2][None], ffn_dn[layer // 2][None], one_tile, all_valid,
                      residual=True, tm=1024, tf=256)
        else:
            i = layer // 2
            xf, route_i, route_w, counts = _merge(*margs, router=(g2, router[i]))
            xf = _moe(xf, g2, route_i, route_w, counts, moe_gu[i], moe_dn[i])
    return xf.reshape(batch, seq_len, d)
```

```python
import functools

import numpy as np
import jax
import jax.numpy as jnp
from jax import lax
from jax.experimental import pallas as pl
from jax.experimental.pallas import tpu as pltpu

D_MODEL = 1024
HEAD_DIM = 64
GRID_W = 64
EPS = 1e-6
ROPE_THETA = 10000.0
NEG_INF = -1e30
LOG2_E = 1.4426950408889634
NA_HEADS = 4
NA_WIN_ROWS = 8
NA_WIN_COLS = 16
GQA_Q_HEADS = 6
GQA_KV_HEADS = 2
DILATIONS = (1, 4, 16)
DIL_SIDE = 64
QA_W, QB_W, QC_W = 256, 384, 384
KA_W, KB_W, KC_W = 256, 128, 384
Q_W = QA_W + QB_W + QC_W
KV_W = KA_W + KB_W + KC_W
QK_W = Q_W + KV_W
QKV_W = Q_W + 2 * KV_W
GATE_W = 3 * D_MODEL
N_EXPERTS = 8
LANES = 128
VMEM_LIMIT = 48 * 1024 * 1024
NA_UNROLL = 4
ROW_DMA_UNROLL = 8

GQA_HEAD_ORDER = (0, 3, 1, 4, 2, 5)

F32 = jnp.float32
BF16 = jnp.bfloat16


def _nt_dot(a, b):
    return lax.dot_general(a, b, (((1,), (1,)), ((), ())), preferred_element_type=F32)


def _rms(x, g):
    ms = jnp.mean(x * x, axis=-1, keepdims=True)
    return x * lax.rsqrt(ms + EPS) * g


_QK_CHUNKS = (
    (0, None, ("qa", 0)), (0, None, ("qa", 1)),
    (2, "axial", ("qb", 0)), (2, "axial", ("qb", 1)), (2, "axial", ("qb", 2)),
    (4, "seq", ("qc", 0)), (4, "seq", ("qc", 1)), (4, "seq", ("qc", 2)),
    (1, None, ("ka", 0)), (1, None, ("ka", 1)),
    (3, "axial", ("kb", 0)),
    (5, "seq", ("kc", 0)), (5, "seq", ("kc", 1)), (5, "seq", ("kc", 2)),
)


def _qkv_kernel(x_ref, g_ref, w_ref, gains_ref, bd_ref, cax_ref, sax_ref, cseq_ref, sseq_ref,
                qa_ref, ka_ref, va_ref, qb_ref, kb_ref, vb_ref, *c_refs):
    qc_refs, kc_refs, vc_refs = c_refs[0:3], c_refs[3:6], c_refs[6:9]
    tm = x_ref.shape[0]
    h = _rms(x_ref[...], g_ref[...]).astype(BF16)
    p = jnp.dot(h, w_ref[...], preferred_element_type=F32)
    lane = lax.broadcasted_iota(jnp.int32, (tm, LANES), 1)
    dests = {"qa": qa_ref, "ka": ka_ref, "qb": qb_ref, "kb": kb_ref, "qc": qc_refs, "kc": kc_refs}
    sumsq = None
    for c, (grow, rope, (dname, dj)) in enumerate(_QK_CHUNKS):
        if c % 2 == 0:
            pc2 = p[:, c * LANES:(c + 2) * LANES]
            sumsq = jnp.dot((pc2 * pc2).astype(BF16), bd_ref[...], preferred_element_type=F32)
        pc = p[:, c * LANES:(c + 1) * LANES]
        sc = sumsq[:, (c % 2) * LANES:(c % 2 + 1) * LANES]
        y = pc * lax.rsqrt(sc * (1.0 / HEAD_DIM) + EPS) * gains_ref[grow:grow + 1, :]
        if rope is not None:
            half = 16 if rope == "axial" else 32
            cos_t = cax_ref[...] if rope == "axial" else cseq_ref[...]
            sin_t = sax_ref[...] if rope == "axial" else sseq_ref[...]
            partner = jnp.where((lane % (2 * half)) < half,
                                pltpu.roll(y, LANES - half, axis=1), pltpu.roll(y, half, axis=1))
            y = y * cos_t + partner * sin_t
        yb = y.astype(BF16)
        d = dests[dname]
        if dname in ("qc", "kc"):
            d[dj][...] = yb
        else:
            d[:, dj * LANES:(dj + 1) * LANES] = yb
    v = p[:, QK_W:].astype(BF16)
    va_ref[...] = v[:, :KA_W]
    vb_ref[...] = v[:, KA_W:KA_W + KB_W]
    for j in range(3):
        vc_refs[j][...] = v[:, KA_W + KB_W + j * LANES:KA_W + KB_W + (j + 1) * LANES]


def _qkv_proj(x, g, w, gains, bd, tabs, seq_len, tm=512):
    n = x.shape[0]
    nb = seq_len // tm
    row = lambda i: (i, 0)
    const = lambda i: (0, 0)
    tab = lambda i: (i % nb, 0)
    widths = (QA_W, KA_W, KA_W, QB_W, KB_W, KB_W) + (LANES,) * 9
    out_shapes = [jax.ShapeDtypeStruct((n, w_), BF16) for w_ in widths]
    out_specs = [pl.BlockSpec((tm, w_), row) for w_ in widths]
    return pl.pallas_call(
        _qkv_kernel,
        grid=(n // tm,),
        in_specs=[pl.BlockSpec((tm, D_MODEL), row), pl.BlockSpec((1, D_MODEL), const),
                  pl.BlockSpec((D_MODEL, QKV_W), const), pl.BlockSpec((8, LANES), const),
                  pl.BlockSpec((2 * LANES, 2 * LANES), const)] + [pl.BlockSpec((tm, LANES), tab)] * 4,
        out_specs=out_specs,
        out_shape=out_shapes,
        compiler_params=pltpu.CompilerParams(dimension_semantics=("parallel",), vmem_limit_bytes=VMEM_LIMIT),
        name="qkv_proj",
    )(x, g, w, gains, bd, *tabs)


def _na_kernel(q_ref, k_ref, v_ref, bias_ref, o_ref, *, rows_per_step):
    rg = pl.program_id(1)
    n_rows = k_ref.shape[0] // GRID_W
    nk = NA_WIN_ROWS * GRID_W
    lo = lax.broadcasted_iota(jnp.int32, (GRID_W, LANES), 1) < HEAD_DIM

    def body(rr, carry):
        r = rg * rows_per_step + rr
        row_start = jnp.clip(r - NA_WIN_ROWS // 2, 0, n_rows - NA_WIN_ROWS)
        off = r - row_start
        ks = pl.multiple_of(row_start * GRID_W, GRID_W)
        qs = pl.multiple_of(rr * GRID_W, GRID_W)
        qrow = q_ref[pl.ds(qs, GRID_W), :]
        outs = []
        for p in range(NA_HEADS // 2):
            q2 = qrow[:, p * LANES:(p + 1) * LANES]
            z = jnp.zeros_like(q2)
            lhs = jnp.concatenate([jnp.where(lo, q2, z), jnp.where(lo, z, q2)], axis=0)
            kblk = k_ref[pl.ds(ks, nk), p * LANES:(p + 1) * LANES]
            vblk = v_ref[pl.ds(ks, nk), p * LANES:(p + 1) * LANES]
            s = _nt_dot(lhs, kblk)
            s = s + jnp.concatenate([bias_ref[2 * p, off], bias_ref[2 * p + 1, off]], axis=0)
            m = jnp.max(s, axis=-1, keepdims=True)
            e = jnp.exp(s - m)
            l = jnp.sum(e, axis=-1, keepdims=True)
            pv = jnp.dot(e.astype(BF16), vblk, preferred_element_type=F32) / l
            outs.append(jnp.where(lo, pv[:GRID_W], pv[GRID_W:]))
        o_ref[pl.ds(qs, GRID_W), :] = jnp.concatenate(outs, axis=1).astype(o_ref.dtype)
        return carry

    lax.fori_loop(0, rows_per_step, body, 0, unroll=NA_UNROLL)


def _na_bias_table(rpb):
    kw = NA_WIN_COLS
    c = np.arange(GRID_W)
    col_start = np.clip(c - kw // 2, 0, GRID_W - kw)
    col_mask = (c[None, :] >= col_start[:, None]) & (c[None, :] < col_start[:, None] + kw)
    dc = np.clip(c[None, :] - c[:, None], -(kw - 1), kw - 1) + NA_WIN_COLS - 1
    onehot = jnp.asarray(dc[None, :, :] == np.arange(2 * kw - 1)[:, None, None], F32)
    by_col = jnp.einsum("...rc,cqk->...rqk", rpb.astype(F32), onehot, precision=lax.Precision.HIGHEST)
    top = NA_WIN_ROWS - 1
    b = jnp.stack([by_col[..., top - off:top - off + NA_WIN_ROWS, :, :] for off in range(NA_WIN_ROWS)], axis=-4)
    b = jnp.swapaxes(b, -3, -2)
    b = jnp.where(col_mask[:, None, :], b, NEG_INF)
    return b.reshape(*rpb.shape[:-2], NA_WIN_ROWS, GRID_W, NA_WIN_ROWS * GRID_W)


def _na_attention(q, k, v, bias, batch, seq_len, rows_per_step=8):
    q3, k3, v3 = (t.reshape(batch, seq_len, QA_W) for t in (q, k, v))
    tq = rows_per_step * GRID_W
    out = pl.pallas_call(
        functools.partial(_na_kernel, rows_per_step=rows_per_step),
        grid=(batch, seq_len // tq),
        in_specs=[pl.BlockSpec((None, tq, QA_W), lambda b, i: (b, i, 0)),
                  pl.BlockSpec((None, seq_len, QA_W), lambda b, i: (b, 0, 0)),
                  pl.BlockSpec((None, seq_len, QA_W), lambda b, i: (b, 0, 0)),
                  pl.BlockSpec(bias.shape, lambda b, i: (0, 0, 0, 0))],
        out_specs=pl.BlockSpec((None, tq, QA_W), lambda b, i: (b, i, 0)),
        out_shape=jax.ShapeDtypeStruct((batch, seq_len, QA_W), BF16),
        compiler_params=pltpu.CompilerParams(dimension_semantics=("parallel", "parallel"),
                                             vmem_limit_bytes=VMEM_LIMIT),
        name="na_attn",
    )(q3, k3, v3, bias)
    return out.reshape(batch * seq_len, QA_W)


def _gqa_kernel(q_ref, k_ref, v_ref, o_ref, *, tk):
    tq = q_ref.shape[0]
    n_pairs = QB_W // LANES
    lo = lax.broadcasted_iota(jnp.int32, (tq, LANES), 1) < HEAD_DIM
    q = q_ref[...]
    z = jnp.zeros((tq, LANES), q.dtype)
    pairs = [q[:, p * LANES:(p + 1) * LANES] for p in range(n_pairs)]
    lhs = jnp.concatenate([jnp.where(lo, t, z) for t in pairs] + [jnp.where(lo, z, t) for t in pairs], axis=0)
    rows = lhs.shape[0]

    def body(kc, carry):
        m, l, acc = carry
        ks = pl.multiple_of(kc * tk, tk)
        kblk = k_ref[pl.ds(ks, tk), :]
        vblk = v_ref[pl.ds(ks, tk), :]
        s = _nt_dot(lhs, kblk)
        m_new = jnp.maximum(m, jnp.max(s, axis=-1, keepdims=True))
        e = jnp.exp2(s - m_new)
        alpha = jnp.exp2(m - m_new)
        l = alpha * l + jnp.sum(e, axis=-1, keepdims=True)
        acc = alpha * acc + jnp.dot(e.astype(BF16), vblk, preferred_element_type=F32)
        return m_new, l, acc

    init = (jnp.full((rows, 1), NEG_INF, F32), jnp.zeros((rows, 1), F32), jnp.zeros((rows, LANES), F32))
    _, l, acc = lax.fori_loop(0, k_ref.shape[0] // tk, body, init)
    o = acc / l
    half = n_pairs * tq
    for p in range(n_pairs):
        o_ref[:, p * LANES:(p + 1) * LANES] = jnp.where(
            lo, o[p * tq:(p + 1) * tq], o[half + p * tq:half + (p + 1) * tq]).astype(o_ref.dtype)


def _gqa_attention(q, k, v, batch, seq_len, tq=128, tk=2048):
    q3 = q.reshape(batch, seq_len, QB_W)
    k3 = k.reshape(batch, seq_len, KB_W)
    v3 = v.reshape(batch, seq_len, KB_W)
    out = pl.pallas_call(
        functools.partial(_gqa_kernel, tk=tk),
        grid=(batch, seq_len // tq),
        in_specs=[pl.BlockSpec((None, tq, QB_W), lambda b, i: (b, i, 0)),
                  pl.BlockSpec((None, seq_len, KB_W), lambda b, i: (b, 0, 0)),
                  pl.BlockSpec((None, seq_len, KB_W), lambda b, i: (b, 0, 0))],
        out_specs=pl.BlockSpec((None, tq, QB_W), lambda b, i: (b, i, 0)),
        out_shape=jax.ShapeDtypeStruct((batch, seq_len, QB_W), BF16),
        compiler_params=pltpu.CompilerParams(dimension_semantics=("parallel", "parallel"),
                                             vmem_limit_bytes=VMEM_LIMIT),
        name="gqa_attn",
    )(q3, k3, v3)
    return out.reshape(batch * seq_len, QB_W)


def _dil_kernel(q_ref, k_ref, v_ref, o_ref, lse_ref):
    sub_len = q_ref.shape[0]
    tq = 2 * DIL_SIDE
    tkk = 4 * DIL_SIDE
    lo = lax.broadcasted_iota(jnp.int32, (tq, LANES), 1) < HEAD_DIM
    qi_local = lax.broadcasted_iota(jnp.int32, (2 * tq, tkk), 0) % tq
    ki_local = lax.broadcasted_iota(jnp.int32, (2 * tq, tkk), 1)

    def body(qb, carry):
        i0 = pl.multiple_of(qb * tq, tq)
        ks = pl.multiple_of(jnp.clip(i0 - DIL_SIDE, 0, sub_len - tkk), DIL_SIDE)
        q2 = q_ref[pl.ds(i0, tq), :]
        z = jnp.zeros_like(q2)
        lhs = jnp.concatenate([jnp.where(lo, q2, z), jnp.where(lo, z, q2)], axis=0)
        s = _nt_dot(lhs, k_ref[pl.ds(ks, tkk), :])
        delta = (qi_local + i0) - (ki_local + ks)
        s = jnp.where((delta <= DIL_SIDE) & (delta >= -DIL_SIDE), s, NEG_INF)
        m = jnp.max(s, axis=-1, keepdims=True)
        e = jnp.exp(s - m)
        l = jnp.sum(e, axis=-1, keepdims=True)
        pv = jnp.dot(e.astype(BF16), v_ref[pl.ds(ks, tkk), :], preferred_element_type=F32) / l
        lse = m + jnp.log(l)
        o_ref[pl.ds(i0, tq), :] = jnp.where(lo, pv[:tq], pv[tq:])
        lse_ref[pl.ds(i0, tq), :] = jnp.where(lo, lse[:tq], lse[tq:])
        return carry

    lax.fori_loop(0, sub_len // tq, body, 0, unroll=2)


def _dil_attention(q, k, v, dil, batch, seq_len):
    sub_len = seq_len // dil
    view = lambda t: t.reshape(batch, sub_len, dil * LANES)
    spec = pl.BlockSpec((None, sub_len, LANES), lambda b, r: (b, 0, r))
    o, lse = pl.pallas_call(
        _dil_kernel,
        grid=(batch, dil),
        in_specs=[spec, spec, spec],
        out_specs=[spec, spec],
        out_shape=[jax.ShapeDtypeStruct((batch, sub_len, dil * LANES), F32)] * 2,
        compiler_params=pltpu.CompilerParams(dimension_semantics=("parallel", "parallel"),
                                             vmem_limit_bytes=VMEM_LIMIT),
        name=f"dil_attn_d{dil}",
    )(view(q), view(k), view(v))
    n = batch * seq_len
    return o.reshape(n, LANES), lse.reshape(n, LANES)


def _merge_body(x_ref, g_ref, oa_ref, ob_ref, oc0_ref, oc1_ref, oc2_ref, lse0_ref, lse1_ref, lse2_ref,
                wg_ref, wba_ref, wbb_ref, wbc_ref, wo_ref):
    x = x_ref[...]
    h = _rms(x, g_ref[...]).astype(BF16)
    lse = [lse0_ref[...], lse1_ref[...], lse2_ref[...]]
    mx = jnp.maximum(jnp.maximum(lse[0], lse[1]), lse[2])
    ex = [jnp.exp(t - mx) for t in lse]
    den = ex[0] + ex[1] + ex[2]
    oc = (ex[0] / den) * oc0_ref[...] + (ex[1] / den) * oc1_ref[...] + (ex[2] / den) * oc2_ref[...]
    ya = jnp.dot(oa_ref[...], wba_ref[...], preferred_element_type=F32)
    yb = jnp.dot(ob_ref[...], wbb_ref[...], preferred_element_type=F32)
    yc = jnp.dot(oc.astype(BF16), wbc_ref[...], preferred_element_type=F32)
    merged = None
    for j, y in enumerate((ya, yb, yc)):
        gate = jnp.dot(h, wg_ref[:, j * D_MODEL:(j + 1) * D_MODEL], preferred_element_type=F32)
        t = jax.nn.sigmoid(gate) * y
        merged = t if merged is None else merged + t
    return x + jnp.dot(merged.astype(BF16), wo_ref[...], preferred_element_type=F32)


def _merge_kernel(*refs):
    *ins, out_ref = refs
    out_ref[...] = _merge_body(*ins)


def _merge_route_kernel(*refs):
    *ins, g2_ref, wrh_ref, wrl_ref, tri_ref, out_ref, ri_ref, rw_ref, cnt_ref, carry_ref = refs
    xn = _merge_body(*ins)
    out_ref[...] = xn
    tm = xn.shape[0]

    @pl.when(pl.program_id(0) == 0)
    def _():
        carry_ref[...] = jnp.zeros_like(carry_ref)

    h2 = _rms(xn, g2_ref[...])
    h_hi = h2.astype(BF16)
    h_lo = (h2 - h_hi.astype(F32)).astype(BF16)
    logits = (jnp.dot(h_hi, wrh_ref[...], preferred_element_type=F32)
              + jnp.dot(h_lo, wrh_ref[...], preferred_element_type=F32)
              + jnp.dot(h_hi, wrl_ref[...], preferred_element_type=F32))
    lt = logits.T[:N_EXPERTS]
    row = lax.broadcasted_iota(jnp.int32, (N_EXPERTS, tm), 0)
    row_f = row.astype(F32)
    v1 = jnp.max(lt, axis=0, keepdims=True)
    i1 = jnp.min(jnp.where(lt == v1, row_f, float(N_EXPERTS)), axis=0, keepdims=True)
    hot1 = row_f == i1
    rest = jnp.where(hot1, -jnp.inf, lt)
    v2 = jnp.max(rest, axis=0, keepdims=True)
    i2 = jnp.min(jnp.where(rest == v2, row_f, float(N_EXPERTS)), axis=0, keepdims=True)
    hot2 = row_f == i2
    e2 = jnp.exp(v2 - v1)
    w1 = 1.0 / (1.0 + e2)
    w2 = e2 / (1.0 + e2)
    hot = jnp.where(hot1 | hot2, 1.0, 0.0)
    hot16 = jnp.concatenate([hot, jnp.zeros_like(hot)], axis=0).astype(BF16)
    carry = carry_ref[:, 0:1]
    before = jnp.dot(hot16, tri_ref[...], preferred_element_type=F32)[:N_EXPERTS] + carry
    r1 = jnp.sum(jnp.where(hot1, before, 0.0), axis=0, keepdims=True)
    r2 = jnp.sum(jnp.where(hot2, before, 0.0), axis=0, keepdims=True)
    total = carry + jnp.sum(hot, axis=1, keepdims=True)
    carry_ref[...] = jnp.broadcast_to(total, carry_ref.shape)
    cnt_ref[...] = jnp.broadcast_to(total, cnt_ref.shape)
    ri = jnp.where(row == 0, i1, jnp.where(row == 1, i2, jnp.where(row == 2, r1, r2)))
    ri_ref[...] = ri.astype(jnp.int32)
    rw_ref[...] = jnp.where(row == 0, w1, jnp.where(row == 1, w2, 0.0))


def _merge(x, g1, oa, ob, oc, lse, wg, wba, wbb, wbc, wo, router=None, tm=512):
    n = x.shape[0]
    row = lambda i: (i, 0)
    const = lambda i: (0, 0)
    in_specs = [pl.BlockSpec((tm, D_MODEL), row), pl.BlockSpec((1, D_MODEL), const),
                pl.BlockSpec((tm, QA_W), row), pl.BlockSpec((tm, QB_W), row)]
    in_specs += [pl.BlockSpec((tm, LANES), row)] * 6
    in_specs += [pl.BlockSpec((D_MODEL, GATE_W), const), pl.BlockSpec((QA_W, D_MODEL), const),
                 pl.BlockSpec((QB_W, D_MODEL), const), pl.BlockSpec((LANES, D_MODEL), const),
                 pl.BlockSpec((D_MODEL, D_MODEL), const)]
    args = [x, g1, oa, ob, *oc, *lse, wg, wba, wbb, wbc, wo]
    x_shape = jax.ShapeDtypeStruct((n, D_MODEL), F32)
    x_spec = pl.BlockSpec((tm, D_MODEL), row)
    if router is None:
        return pl.pallas_call(
            _merge_kernel, grid=(n // tm,), in_specs=in_specs, out_specs=x_spec, out_shape=x_shape,
            compiler_params=pltpu.CompilerParams(dimension_semantics=("parallel",), vmem_limit_bytes=VMEM_LIMIT),
            name="merge",
        )(*args)
    g2, wr_hi, wr_lo = router
    tri = jnp.triu(jnp.ones((tm, tm), BF16), k=1)
    col = lambda i: (0, i)
    in_specs += [pl.BlockSpec((1, D_MODEL), const), pl.BlockSpec((D_MODEL, LANES), const),
                 pl.BlockSpec((D_MODEL, LANES), const), pl.BlockSpec((tm, tm), const)]
    return pl.pallas_call(
        _merge_route_kernel, grid=(n // tm,), in_specs=in_specs,
        out_specs=[x_spec, pl.BlockSpec((N_EXPERTS, tm), col), pl.BlockSpec((N_EXPERTS, tm), col),
                   pl.BlockSpec((N_EXPERTS, LANES), const)],
        out_shape=[x_shape, jax.ShapeDtypeStruct((N_EXPERTS, n), jnp.int32),
                   jax.ShapeDtypeStruct((N_EXPERTS, n), F32), jax.ShapeDtypeStruct((N_EXPERTS, LANES), F32)],
        scratch_shapes=[pltpu.VMEM((N_EXPERTS, LANES), F32)],
        compiler_params=pltpu.CompilerParams(dimension_semantics=("arbitrary",), vmem_limit_bytes=VMEM_LIMIT),
        name="merge_route",
    )(*args, g2, wr_hi, wr_lo, tri)


def _ffn_kernel(te_ref, tv_ref, x_ref, g_ref, wg_ref, wu_ref, wd_ref, o_ref, h_ref, acc_ref, *, residual):
    i = pl.program_id(0)
    f = pl.program_id(1)

    @pl.when(f == 0)
    def _():
        h_ref[...] = _rms(x_ref[...], g_ref[...]).astype(BF16)
        acc_ref[...] = jnp.zeros_like(acc_ref)

    @pl.when(tv_ref[i] == 1)
    def _():
        h = h_ref[...]
        gg = jnp.dot(h, wg_ref[...], preferred_element_type=F32)
        uu = jnp.dot(h, wu_ref[...], preferred_element_type=F32)
        a = (jax.nn.silu(gg) * uu).astype(BF16)
        acc_ref[...] += jnp.dot(a, wd_ref[...], preferred_element_type=F32)

    @pl.when(f == pl.num_programs(1) - 1)
    def _():
        o_ref[...] = (x_ref[...] + acc_ref[...]) if residual else acc_ref[...]


def _ffn(x, g, w_gu, w_down, tile_expert, tile_valid, *, residual, tm, tf):
    rows = x.shape[0]
    f_dim = w_down.shape[1]
    nf = f_dim // tf
    fe = lambda f, tv, i: f * tv[i] + (nf - 1) * (1 - tv[i])
    grid_spec = pltpu.PrefetchScalarGridSpec(
        num_scalar_prefetch=2,
        grid=(rows // tm, nf),
        in_specs=[pl.BlockSpec((tm, D_MODEL), lambda i, f, te, tv: (i, 0)),
                  pl.BlockSpec((1, D_MODEL), lambda i, f, te, tv: (0, 0)),
                  pl.BlockSpec((None, D_MODEL, tf), lambda i, f, te, tv: (te[i], 0, fe(f, tv, i))),
                  pl.BlockSpec((None, D_MODEL, tf), lambda i, f, te, tv: (te[i], 0, nf + fe(f, tv, i))),
                  pl.BlockSpec((None, tf, D_MODEL), lambda i, f, te, tv: (te[i], fe(f, tv, i), 0))],
        out_specs=pl.BlockSpec((tm, D_MODEL), lambda i, f, te, tv: (i, 0)),
        scratch_shapes=[pltpu.VMEM((tm, D_MODEL), BF16), pltpu.VMEM((tm, D_MODEL), F32)],
    )
    return pl.pallas_call(
        functools.partial(_ffn_kernel, residual=residual),
        grid_spec=grid_spec,
        out_shape=jax.ShapeDtypeStruct((rows, D_MODEL), F32),
        compiler_params=pltpu.CompilerParams(dimension_semantics=("parallel", "arbitrary"),
                                             vmem_limit_bytes=VMEM_LIMIT),
        name="ffn_res" if residual else "ffn_moe",
    )(tile_expert, tile_valid, x, g, w_gu, w_gu, w_down)


def _dispatch_kernel(pos_ref, x_ref, zeros_ref, xs_ref, sem):
    del zeros_ref
    td = x_ref.shape[0]

    def row_copy(t, slot):
        return pltpu.make_async_copy(x_ref.at[pl.ds(t, 1)], xs_ref.at[pl.ds(pos_ref[0, 2 * t + slot], 1)], sem)

    def start(t, carry):
        row_copy(t, 0).start()
        row_copy(t, 1).start()
        return carry

    def wait(t, carry):
        row_copy(t, 0).wait()
        row_copy(t, 1).wait()
        return carry

    lax.fori_loop(0, td, start, 0, unroll=ROW_DMA_UNROLL)
    lax.fori_loop(0, td, wait, 0, unroll=ROW_DMA_UNROLL)


def _dispatch(x, pos, rows_out, td=256):
    n = x.shape[0]
    pos3 = pos.reshape(n // td, 1, 2 * td)
    zeros = jnp.zeros((rows_out, D_MODEL), F32)
    return pl.pallas_call(
        _dispatch_kernel,
        grid=(n // td,),
        in_specs=[pl.BlockSpec((None, 1, 2 * td), lambda i: (i, 0, 0), memory_space=pltpu.SMEM),
                  pl.BlockSpec((td, D_MODEL), lambda i: (i, 0)),
                  pl.BlockSpec(memory_space=pl.ANY)],
        out_specs=pl.BlockSpec(memory_space=pl.ANY),
        out_shape=jax.ShapeDtypeStruct((rows_out, D_MODEL), F32),
        scratch_shapes=[pltpu.SemaphoreType.DMA(())],
        input_output_aliases={2: 0},
        compiler_params=pltpu.CompilerParams(dimension_semantics=("arbitrary",), vmem_limit_bytes=VMEM_LIMIT),
        name="moe_dispatch",
    )(pos3, x, zeros)


def _combine_kernel(pos_ref, x_ref, w_ref, ys_ref, o_ref, buf_ref, sem):
    tc = x_ref.shape[0]

    def row_copy(t, slot):
        return pltpu.make_async_copy(ys_ref.at[pl.ds(pos_ref[0, 2 * t + slot], 1)],
                                     buf_ref.at[slot, pl.ds(t, 1)], sem)

    def start(t, carry):
        row_copy(t, 0).start()
        row_copy(t, 1).start()
        return carry

    def wait(t, carry):
        row_copy(t, 0).wait()
        row_copy(t, 1).wait()
        return carry

    lax.fori_loop(0, tc, start, 0, unroll=ROW_DMA_UNROLL)
    lax.fori_loop(0, tc, wait, 0, unroll=ROW_DMA_UNROLL)
    w = w_ref[...]
    o_ref[...] = x_ref[...] + (w[:, 0:1] * buf_ref[0] + w[:, 1:2] * buf_ref[1])


def _combine(x, w, ys, pos, tc=256):
    n = x.shape[0]
    pos3 = pos.reshape(n // tc, 1, 2 * tc)
    return pl.pallas_call(
        _combine_kernel,
        grid=(n // tc,),
        in_specs=[pl.BlockSpec((None, 1, 2 * tc), lambda i: (i, 0, 0), memory_space=pltpu.SMEM),
                  pl.BlockSpec((tc, D_MODEL), lambda i: (i, 0)),
                  pl.BlockSpec((tc, LANES), lambda i: (i, 0)),
                  pl.BlockSpec(memory_space=pl.ANY)],
        out_specs=pl.BlockSpec((tc, D_MODEL), lambda i: (i, 0)),
        out_shape=jax.ShapeDtypeStruct((n, D_MODEL), F32),
        scratch_shapes=[pltpu.VMEM((2, tc, D_MODEL), F32), pltpu.SemaphoreType.DMA(())],
        compiler_params=pltpu.CompilerParams(dimension_semantics=("arbitrary",), vmem_limit_bytes=VMEM_LIMIT),
        name="moe_combine",
    )(pos3, x, w, ys)


def _moe(x, g2, route_i, route_w, counts, w_gu, w_down, tm=1024, tf=512):
    n = x.shape[0]
    n_tiles = (2 * n) // tm + N_EXPERTS
    cnt = counts[:, 0].astype(jnp.int32)
    padded = ((cnt + tm - 1) // tm) * tm
    off_end = jnp.cumsum(padded)
    off_start = off_end - padded
    experts = route_i[0:2].T
    pos = off_start[experts] + route_i[2:4].T
    weights = jnp.pad(route_w[0:2].T, ((0, 0), (0, LANES - 2)))
    tile_start = jnp.arange(n_tiles, dtype=jnp.int32) * tm
    tile_valid = (tile_start < off_end[-1]).astype(jnp.int32)
    tile_expert = jnp.minimum(jnp.sum(tile_start[:, None] >= off_end[None, :], axis=1), N_EXPERTS - 1)
    last_expert = tile_expert[jnp.maximum(off_end[-1] // tm - 1, 0)]
    tile_expert = jnp.where(tile_valid == 1, tile_expert, last_expert).astype(jnp.int32)
    xs = _dispatch(x, pos, n_tiles * tm)
    ys = _ffn(xs, g2, w_gu, w_down, tile_expert, tile_valid, residual=False, tm=tm, tf=tf)
    return _combine(x, weights, ys, pos)


def _rope_tables(seq_len):
    def angles(pos, dim):
        inv_freq = ROPE_THETA ** (-jnp.arange(0, dim, 2, dtype=F32) / dim)
        return pos.astype(F32)[:, None] * inv_freq[None, :]

    pos = jnp.arange(seq_len)
    a_seq = angles(pos, HEAD_DIM)
    a_row = angles(pos // GRID_W, HEAD_DIM // 2)
    a_col = angles(pos % GRID_W, HEAD_DIM // 2)
    cos_seq = jnp.concatenate([jnp.cos(a_seq)] * 2, axis=-1)
    sin_seq = jnp.concatenate([-jnp.sin(a_seq), jnp.sin(a_seq)], axis=-1)
    cos_ax = jnp.concatenate([jnp.cos(a_row)] * 2 + [jnp.cos(a_col)] * 2, axis=-1)
    sin_ax = jnp.concatenate([-jnp.sin(a_row), jnp.sin(a_row), -jnp.sin(a_col), jnp.sin(a_col)], axis=-1)
    two = lambda t: jnp.concatenate([t, t], axis=-1)
    return two(cos_ax), two(sin_ax), two(cos_seq), two(sin_seq)


def _qkv_column_order():
    cols = np.arange(QKV_W)
    qb = QA_W + np.concatenate([np.arange(h * HEAD_DIM, (h + 1) * HEAD_DIM) for h in GQA_HEAD_ORDER])
    cols[QA_W:QA_W + QB_W] = qb
    return cols


def _branch_row_order():
    rows = np.arange(QA_W + QB_W + LANES)
    rows[QA_W:QA_W + QB_W] = QA_W + np.concatenate(
        [np.arange(h * HEAD_DIM, (h + 1) * HEAD_DIM) for h in GQA_HEAD_ORDER])
    return rows


def kernel(x, norm_gains, w_in, qk_gains, na_rpb, w_branch, w_out, ffn_w_gu, ffn_w_down,
           moe_router, moe_w_gu, moe_w_down):
    batch, seq_len, d = x.shape
    depth = w_in.shape[0]
    n = batch * seq_len
    assert d == D_MODEL and seq_len == GRID_W * GRID_W

    tabs = _rope_tables(seq_len)
    na_bias = _na_bias_table(na_rpb)
    w_qkv = w_in[:, :, _qkv_column_order()].astype(BF16)
    w_gate = w_in[:, :, QKV_W:].astype(BF16)
    w_br = w_branch[:, _branch_row_order(), :].astype(BF16)
    w_o = w_out.astype(BF16)
    ffn_gu = ffn_w_gu.astype(BF16)
    ffn_dn = ffn_w_down.astype(BF16)
    moe_gu = moe_w_gu.astype(BF16)
    moe_dn = moe_w_down.astype(BF16)
    router = jnp.pad(moe_router, ((0, 0), (0, 0), (0, LANES - N_EXPERTS)))
    router_hi = router.astype(BF16)
    router_lo = (router - router_hi.astype(F32)).astype(BF16)
    gains = jnp.tile(qk_gains.reshape(depth, 6, HEAD_DIM), (1, 1, 2))
    q_scale = HEAD_DIM ** -0.5
    gains = gains * jnp.asarray([q_scale, 1.0, q_scale * LOG2_E, 1.0, q_scale, 1.0], F32)[None, :, None]
    gains = jnp.pad(gains, ((0, 0), (0, 2), (0, 0)))
    hd_idx = np.arange(2 * LANES) // HEAD_DIM
    block_ones = jnp.asarray(hd_idx[:, None] == hd_idx[None, :], BF16)
    one_tile = jnp.zeros((n // 1024,), jnp.int32)
    all_valid = jnp.ones((n // 1024,), jnp.int32)

    xf = x.reshape(n, d)
    for layer in range(depth):
        g1 = norm_gains[layer, 0][None, :]
        g2 = norm_gains[layer, 1][None, :]
        qa, ka, va, qb, kb, vb, *cs = _qkv_proj(xf, g1, w_qkv[layer], gains[layer], block_ones, tabs, seq_len)
        qc, kc, vc = cs[0:3], cs[3:6], cs[6:9]
        oa = _na_attention(qa, ka, va, na_bias[layer], batch, seq_len)
        ob = _gqa_attention(qb, kb, vb, batch, seq_len)
        dil = [_dil_attention(qc[j], kc[j], vc[j], DILATIONS[j], batch, seq_len) for j in range(3)]
        oc = [t[0] for t in dil]
        lse = [t[1] for t in dil]
        wb = w_br[layer]
        margs = (xf, g1, oa, ob, oc, lse, w_gate[layer], wb[:QA_W], wb[QA_W:QA_W + QB_W], wb[QA_W + QB_W:], w_o[layer])
        if layer % 2 == 0:
            xf = _merge(*margs)
            xf = _ffn(xf, g2, ffn_gu[layer // 2][None], ffn_dn[layer // 2][None], one_tile, all_valid,
                      residual=True, tm=1024, tf=256)
        else:
            i = layer // 2
            xf, route_i, route_w, counts = _merge(*margs, router=(g2, router_hi[i], router_lo[i]))
            xf = _moe(xf, g2, route_i, route_w, counts, moe_gu[i], moe_dn[i])
    return xf.reshape(batch, seq_len, d)
```

```python
import functools

import numpy as np
import jax
import jax.numpy as jnp
from jax import lax
from jax.experimental import pallas as pl
from jax.experimental.pallas import tpu as pltpu

D_MODEL = 1024
HEAD_DIM = 64
GRID_W = 64
EPS = 1e-6
ROPE_THETA = 10000.0
NEG_INF = -1e30
LOG2_E = 1.4426950408889634
NA_HEADS = 4
NA_WIN_ROWS = 8
NA_WIN_COLS = 16
GQA_Q_HEADS = 6
GQA_KV_HEADS = 2
DILATIONS = (1, 4, 16)
DIL_SIDE = 64
QA_W, QB_W, QC_W = 256, 384, 384
KA_W, KB_W, KC_W = 256, 128, 384
Q_W = QA_W + QB_W + QC_W
KV_W = KA_W + KB_W + KC_W
QK_W = Q_W + KV_W
QKV_W = Q_W + 2 * KV_W
GATE_W = 3 * D_MODEL
N_EXPERTS = 8
LANES = 128
VMEM_LIMIT = 48 * 1024 * 1024
NA_UNROLL = 8
ROW_DMA_UNROLL = 8

GQA_HEAD_ORDER = (0, 3, 1, 4, 2, 5)

F32 = jnp.float32
BF16 = jnp.bfloat16


def _nt_dot(a, b):
    return lax.dot_general(a, b, (((1,), (1,)), ((), ())), preferred_element_type=F32)


def _rms(x, g):
    ms = jnp.mean(x * x, axis=-1, keepdims=True)
    return x * lax.rsqrt(ms + EPS) * g


_QK_CHUNKS = (
    (0, None, ("qa", 0)), (0, None, ("qa", 1)),
    (2, "axial", ("qb", 0)), (2, "axial", ("qb", 1)), (2, "axial", ("qb", 2)),
    (4, "seq", ("qc", 0)), (4, "seq", ("qc", 1)), (4, "seq", ("qc", 2)),
    (1, None, ("ka", 0)), (1, None, ("ka", 1)),
    (3, "axial", ("kb", 0)),
    (5, "seq", ("kc", 0)), (5, "seq", ("kc", 1)), (5, "seq", ("kc", 2)),
)


def _qkv_kernel(x_ref, g_ref, w_ref, gains_ref, bd_ref, cax_ref, sax_ref, cseq_ref, sseq_ref,
                qa_ref, ka_ref, va_ref, qb_ref, kb_ref, vb_ref, *c_refs):
    qc_refs, kc_refs, vc_refs = c_refs[0:3], c_refs[3:6], c_refs[6:9]
    tm = x_ref.shape[0]
    h = _rms(x_ref[...], g_ref[...]).astype(BF16)
    p = jnp.dot(h, w_ref[...], preferred_element_type=F32)
    lane = lax.broadcasted_iota(jnp.int32, (tm, LANES), 1)
    dests = {"qa": qa_ref, "ka": ka_ref, "qb": qb_ref, "kb": kb_ref, "qc": qc_refs, "kc": kc_refs}
    sumsq = None
    for c, (grow, rope, (dname, dj)) in enumerate(_QK_CHUNKS):
        if c % 2 == 0:
            pc2 = p[:, c * LANES:(c + 2) * LANES]
            sumsq = jnp.dot((pc2 * pc2).astype(BF16), bd_ref[...], preferred_element_type=F32)
        pc = p[:, c * LANES:(c + 1) * LANES]
        sc = sumsq[:, (c % 2) * LANES:(c % 2 + 1) * LANES]
        y = pc * lax.rsqrt(sc * (1.0 / HEAD_DIM) + EPS) * gains_ref[grow:grow + 1, :]
        if rope is not None:
            half = 16 if rope == "axial" else 32
            cos_t = cax_ref[...] if rope == "axial" else cseq_ref[...]
            sin_t = sax_ref[...] if rope == "axial" else sseq_ref[...]
            partner = jnp.where((lane % (2 * half)) < half,
                                pltpu.roll(y, LANES - half, axis=1), pltpu.roll(y, half, axis=1))
            y = y * cos_t + partner * sin_t
        yb = y.astype(BF16)
        d = dests[dname]
        if dname in ("qc", "kc"):
            d[dj][...] = yb
        else:
            d[:, dj * LANES:(dj + 1) * LANES] = yb
    v = p[:, QK_W:].astype(BF16)
    va_ref[...] = v[:, :KA_W]
    vb_ref[...] = v[:, KA_W:KA_W + KB_W]
    for j in range(3):
        vc_refs[j][...] = v[:, KA_W + KB_W + j * LANES:KA_W + KB_W + (j + 1) * LANES]


def _qkv_proj(x, g, w, gains, bd, tabs, seq_len, tm=512):
    n = x.shape[0]
    nb = seq_len // tm
    row = lambda i: (i, 0)
    const = lambda i: (0, 0)
    tab = lambda i: (i % nb, 0)
    widths = (QA_W, KA_W, KA_W, QB_W, KB_W, KB_W) + (LANES,) * 9
    out_shapes = [jax.ShapeDtypeStruct((n, w_), BF16) for w_ in widths]
    out_specs = [pl.BlockSpec((tm, w_), row) for w_ in widths]
    return pl.pallas_call(
        _qkv_kernel,
        grid=(n // tm,),
        in_specs=[pl.BlockSpec((tm, D_MODEL), row), pl.BlockSpec((1, D_MODEL), const),
                  pl.BlockSpec((D_MODEL, QKV_W), const), pl.BlockSpec((8, LANES), const),
                  pl.BlockSpec((2 * LANES, 2 * LANES), const)] + [pl.BlockSpec((tm, LANES), tab)] * 4,
        out_specs=out_specs,
        out_shape=out_shapes,
        compiler_params=pltpu.CompilerParams(dimension_semantics=("parallel",), vmem_limit_bytes=VMEM_LIMIT),
        name="qkv_proj",
    )(x, g, w, gains, bd, *tabs)


def _na_kernel(q_ref, k_ref, v_ref, bias_ref, o_ref, *, rows_per_step):
    rg = pl.program_id(1)
    n_rows = k_ref.shape[0] // GRID_W
    nk = NA_WIN_ROWS * GRID_W
    lo = lax.broadcasted_iota(jnp.int32, (GRID_W, LANES), 1) < HEAD_DIM

    def body(rr, carry):
        r = rg * rows_per_step + rr
        row_start = jnp.clip(r - NA_WIN_ROWS // 2, 0, n_rows - NA_WIN_ROWS)
        off = r - row_start
        ks = pl.multiple_of(row_start * GRID_W, GRID_W)
        qs = pl.multiple_of(rr * GRID_W, GRID_W)
        qrow = q_ref[pl.ds(qs, GRID_W), :]
        outs = []
        for p in range(NA_HEADS // 2):
            q2 = qrow[:, p * LANES:(p + 1) * LANES]
            z = jnp.zeros_like(q2)
            lhs = jnp.concatenate([jnp.where(lo, q2, z), jnp.where(lo, z, q2)], axis=0)
            kblk = k_ref[pl.ds(ks, nk), p * LANES:(p + 1) * LANES]
            vblk = v_ref[pl.ds(ks, nk), p * LANES:(p + 1) * LANES]
            s = _nt_dot(lhs, kblk)
            s = s + jnp.concatenate([bias_ref[2 * p, off], bias_ref[2 * p + 1, off]], axis=0)
            m = jnp.max(s, axis=-1, keepdims=True)
            e = jnp.exp(s - m)
            l = jnp.sum(e, axis=-1, keepdims=True)
            pv = jnp.dot(e.astype(BF16), vblk, preferred_element_type=F32) / l
            outs.append(jnp.where(lo, pv[:GRID_W], pv[GRID_W:]))
        o_ref[pl.ds(qs, GRID_W), :] = jnp.concatenate(outs, axis=1).astype(o_ref.dtype)
        return carry

    lax.fori_loop(0, rows_per_step, body, 0, unroll=NA_UNROLL)


def _na_bias_table(rpb):
    kw = NA_WIN_COLS
    c = np.arange(GRID_W)
    col_start = np.clip(c - kw // 2, 0, GRID_W - kw)
    col_mask = (c[None, :] >= col_start[:, None]) & (c[None, :] < col_start[:, None] + kw)
    dc = np.clip(c[None, :] - c[:, None], -(kw - 1), kw - 1) + NA_WIN_COLS - 1
    onehot = jnp.asarray(dc[None, :, :] == np.arange(2 * kw - 1)[:, None, None], F32)
    by_col = jnp.einsum("...rc,cqk->...rqk", rpb.astype(F32), onehot, precision=lax.Precision.HIGHEST)
    top = NA_WIN_ROWS - 1
    b = jnp.stack([by_col[..., top - off:top - off + NA_WIN_ROWS, :, :] for off in range(NA_WIN_ROWS)], axis=-4)
    b = jnp.swapaxes(b, -3, -2)
    b = jnp.where(col_mask[:, None, :], b, NEG_INF)
    return b.reshape(*rpb.shape[:-2], NA_WIN_ROWS, GRID_W, NA_WIN_ROWS * GRID_W)


def _na_attention(q, k, v, bias, batch, seq_len, rows_per_step=8):
    q3, k3, v3 = (t.reshape(batch, seq_len, QA_W) for t in (q, k, v))
    tq = rows_per_step * GRID_W
    out = pl.pallas_call(
        functools.partial(_na_kernel, rows_per_step=rows_per_step),
        grid=(batch, seq_len // tq),
        in_specs=[pl.BlockSpec((None, tq, QA_W), lambda b, i: (b, i, 0)),
                  pl.BlockSpec((None, seq_len, QA_W), lambda b, i: (b, 0, 0)),
                  pl.BlockSpec((None, seq_len, QA_W), lambda b, i: (b, 0, 0)),
                  pl.BlockSpec(bias.shape, lambda b, i: (0, 0, 0, 0))],
        out_specs=pl.BlockSpec((None, tq, QA_W), lambda b, i: (b, i, 0)),
        out_shape=jax.ShapeDtypeStruct((batch, seq_len, QA_W), BF16),
        compiler_params=pltpu.CompilerParams(dimension_semantics=("parallel", "parallel"),
                                             vmem_limit_bytes=VMEM_LIMIT),
        name="na_attn",
    )(q3, k3, v3, bias)
    return out.reshape(batch * seq_len, QA_W)


def _gqa_kernel(q_ref, k_ref, v_ref, o_ref, *, tk):
    tq = q_ref.shape[0]
    n_pairs = QB_W // LANES
    lo = lax.broadcasted_iota(jnp.int32, (tq, LANES), 1) < HEAD_DIM
    q = q_ref[...]
    z = jnp.zeros((tq, LANES), q.dtype)
    pairs = [q[:, p * LANES:(p + 1) * LANES] for p in range(n_pairs)]
    lhs = jnp.concatenate([jnp.where(lo, t, z) for t in pairs] + [jnp.where(lo, z, t) for t in pairs], axis=0)
    rows = lhs.shape[0]

    def body(kc, carry):
        m, l, acc = carry
        ks = pl.multiple_of(kc * tk, tk)
        kblk = k_ref[pl.ds(ks, tk), :]
        vblk = v_ref[pl.ds(ks, tk), :]
        s = _nt_dot(lhs, kblk)
        m_new = jnp.maximum(m, jnp.max(s, axis=-1, keepdims=True))
        e = jnp.exp2(s - m_new)
        alpha = jnp.exp2(m - m_new)
        l = alpha * l + jnp.sum(e, axis=-1, keepdims=True)
        acc = alpha * acc + jnp.dot(e.astype(BF16), vblk, preferred_element_type=F32)
        return m_new, l, acc

    init = (jnp.full((rows, 1), NEG_INF, F32), jnp.zeros((rows, 1), F32), jnp.zeros((rows, LANES), F32))
    _, l, acc = lax.fori_loop(0, k_ref.shape[0] // tk, body, init)
    o = acc / l
    half = n_pairs * tq
    for p in range(n_pairs):
        o_ref[:, p * LANES:(p + 1) * LANES] = jnp.where(
            lo, o[p * tq:(p + 1) * tq], o[half + p * tq:half + (p + 1) * tq]).astype(o_ref.dtype)


def _gqa_attention(q, k, v, batch, seq_len, tq=128, tk=4096):
    q3 = q.reshape(batch, seq_len, QB_W)
    k3 = k.reshape(batch, seq_len, KB_W)
    v3 = v.reshape(batch, seq_len, KB_W)
    out = pl.pallas_call(
        functools.partial(_gqa_kernel, tk=tk),
        grid=(batch, seq_len // tq),
        in_specs=[pl.BlockSpec((None, tq, QB_W), lambda b, i: (b, i, 0)),
                  pl.BlockSpec((None, seq_len, KB_W), lambda b, i: (b, 0, 0)),
                  pl.BlockSpec((None, seq_len, KB_W), lambda b, i: (b, 0, 0))],
        out_specs=pl.BlockSpec((None, tq, QB_W), lambda b, i: (b, i, 0)),
        out_shape=jax.ShapeDtypeStruct((batch, seq_len, QB_W), BF16),
        compiler_params=pltpu.CompilerParams(dimension_semantics=("parallel", "parallel"),
                                             vmem_limit_bytes=VMEM_LIMIT),
        name="gqa_attn",
    )(q3, k3, v3)
    return out.reshape(batch * seq_len, QB_W)


def _dil_kernel(q_ref, k_ref, v_ref, o_ref, lse_ref):
    sub_len = q_ref.shape[0]
    tq = 2 * DIL_SIDE
    tkk = 4 * DIL_SIDE
    lo = lax.broadcasted_iota(jnp.int32, (tq, LANES), 1) < HEAD_DIM
    qi_local = lax.broadcasted_iota(jnp.int32, (2 * tq, tkk), 0) % tq
    ki_local = lax.broadcasted_iota(jnp.int32, (2 * tq, tkk), 1)

    def body(qb, carry):
        i0 = pl.multiple_of(qb * tq, tq)
        ks = pl.multiple_of(jnp.clip(i0 - DIL_SIDE, 0, sub_len - tkk), DIL_SIDE)
        q2 = q_ref[pl.ds(i0, tq), :]
        z = jnp.zeros_like(q2)
        lhs = jnp.concatenate([jnp.where(lo, q2, z), jnp.where(lo, z, q2)], axis=0)
        s = _nt_dot(lhs, k_ref[pl.ds(ks, tkk), :])
        delta = (qi_local + i0) - (ki_local + ks)
        s = jnp.where((delta <= DIL_SIDE) & (delta >= -DIL_SIDE), s, NEG_INF)
        m = jnp.max(s, axis=-1, keepdims=True)
        e = jnp.exp(s - m)
        l = jnp.sum(e, axis=-1, keepdims=True)
        pv = jnp.dot(e.astype(BF16), v_ref[pl.ds(ks, tkk), :], preferred_element_type=F32) / l
        lse = m + jnp.log(l)
        o_ref[pl.ds(i0, tq), :] = jnp.where(lo, pv[:tq], pv[tq:])
        lse_ref[pl.ds(i0, tq), :] = jnp.where(lo, lse[:tq], lse[tq:])
        return carry

    lax.fori_loop(0, sub_len // tq, body, 0, unroll=min(4, sub_len // tq))


def _dil_attention(q, k, v, dil, batch, seq_len):
    sub_len = seq_len // dil
    view = lambda t: t.reshape(batch, sub_len, dil * LANES)
    spec = pl.BlockSpec((None, sub_len, LANES), lambda b, r: (b, 0, r))
    o, lse = pl.pallas_call(
        _dil_kernel,
        grid=(batch, dil),
        in_specs=[spec, spec, spec],
        out_specs=[spec, spec],
        out_shape=[jax.ShapeDtypeStruct((batch, sub_len, dil * LANES), F32)] * 2,
        compiler_params=pltpu.CompilerParams(dimension_semantics=("parallel", "parallel"),
                                             vmem_limit_bytes=VMEM_LIMIT),
        name=f"dil_attn_d{dil}",
    )(view(q), view(k), view(v))
    n = batch * seq_len
    return o.reshape(n, LANES), lse.reshape(n, LANES)


def _merge_body(x_ref, g_ref, oa_ref, ob_ref, oc0_ref, oc1_ref, oc2_ref, lse0_ref, lse1_ref, lse2_ref,
                wg_ref, wba_ref, wbb_ref, wbc_ref, wo_ref):
    x = x_ref[...]
    h = _rms(x, g_ref[...]).astype(BF16)
    lse = [lse0_ref[...], lse1_ref[...], lse2_ref[...]]
    mx = jnp.maximum(jnp.maximum(lse[0], lse[1]), lse[2])
    ex = [jnp.exp(t - mx) for t in lse]
    den = ex[0] + ex[1] + ex[2]
    oc = (ex[0] / den) * oc0_ref[...] + (ex[1] / den) * oc1_ref[...] + (ex[2] / den) * oc2_ref[...]
    ya = jnp.dot(oa_ref[...], wba_ref[...], preferred_element_type=F32)
    yb = jnp.dot(ob_ref[...], wbb_ref[...], preferred_element_type=F32)
    yc = jnp.dot(oc.astype(BF16), wbc_ref[...], preferred_element_type=F32)
    merged = None
    for j, y in enumerate((ya, yb, yc)):
        gate = jnp.dot(h, wg_ref[:, j * D_MODEL:(j + 1) * D_MODEL], preferred_element_type=F32)
        t = jax.nn.sigmoid(gate) * y
        merged = t if merged is None else merged + t
    return x + jnp.dot(merged.astype(BF16), wo_ref[...], preferred_element_type=F32)


def _merge_kernel(*refs):
    *ins, out_ref = refs
    out_ref[...] = _merge_body(*ins)


def _merge_route_kernel(*refs):
    *ins, g2_ref, wrh_ref, wrl_ref, tri_ref, out_ref, ri_ref, rw_ref, cnt_ref, carry_ref = refs
    xn = _merge_body(*ins)
    out_ref[...] = xn
    tm = xn.shape[0]

    @pl.when(pl.program_id(0) == 0)
    def _():
        carry_ref[...] = jnp.zeros_like(carry_ref)

    h2 = _rms(xn, g2_ref[...])
    h_hi = h2.astype(BF16)
    h_lo = (h2 - h_hi.astype(F32)).astype(BF16)
    logits = (jnp.dot(h_hi, wrh_ref[...], preferred_element_type=F32)
              + jnp.dot(h_lo, wrh_ref[...], preferred_element_type=F32)
              + jnp.dot(h_hi, wrl_ref[...], preferred_element_type=F32))
    lt = logits.T[:N_EXPERTS]
    row = lax.broadcasted_iota(jnp.int32, (N_EXPERTS, tm), 0)
    row_f = row.astype(F32)
    v1 = jnp.max(lt, axis=0, keepdims=True)
    i1 = jnp.min(jnp.where(lt == v1, row_f, float(N_EXPERTS)), axis=0, keepdims=True)
    hot1 = row_f == i1
    rest = jnp.where(hot1, -jnp.inf, lt)
    v2 = jnp.max(rest, axis=0, keepdims=True)
    i2 = jnp.min(jnp.where(rest == v2, row_f, float(N_EXPERTS)), axis=0, keepdims=True)
    hot2 = row_f == i2
    e2 = jnp.exp(v2 - v1)
    w1 = 1.0 / (1.0 + e2)
    w2 = e2 / (1.0 + e2)
    hot = jnp.where(hot1 | hot2, 1.0, 0.0)
    hot16 = jnp.concatenate([hot, jnp.zeros_like(hot)], axis=0).astype(BF16)
    carry = carry_ref[:, 0:1]
    before = jnp.dot(hot16, tri_ref[...], preferred_element_type=F32)[:N_EXPERTS] + carry
    r1 = jnp.sum(jnp.where(hot1, before, 0.0), axis=0, keepdims=True)
    r2 = jnp.sum(jnp.where(hot2, before, 0.0), axis=0, keepdims=True)
    total = carry + jnp.sum(hot, axis=1, keepdims=True)
    carry_ref[...] = jnp.broadcast_to(total, carry_ref.shape)
    cnt_ref[...] = jnp.broadcast_to(total, cnt_ref.shape)
    ri = jnp.where(row == 0, i1, jnp.where(row == 1, i2, jnp.where(row == 2, r1, r2)))
    ri_ref[...] = ri.astype(jnp.int32)
    rw_ref[...] = jnp.where(row == 0, w1, jnp.where(row == 1, w2, 0.0))


def _merge(x, g1, oa, ob, oc, lse, wg, wba, wbb, wbc, wo, router=None, tm=512):
    n = x.shape[0]
    row = lambda i: (i, 0)
    const = lambda i: (0, 0)
    in_specs = [pl.BlockSpec((tm, D_MODEL), row), pl.BlockSpec((1, D_MODEL), const),
                pl.BlockSpec((tm, QA_W), row), pl.BlockSpec((tm, QB_W), row)]
    in_specs += [pl.BlockSpec((tm, LANES), row)] * 6
    in_specs += [pl.BlockSpec((D_MODEL, GATE_W), const), pl.BlockSpec((QA_W, D_MODEL), const),
                 pl.BlockSpec((QB_W, D_MODEL), const), pl.BlockSpec((LANES, D_MODEL), const),
                 pl.BlockSpec((D_MODEL, D_MODEL), const)]
    args = [x, g1, oa, ob, *oc, *lse, wg, wba, wbb, wbc, wo]
    x_shape = jax.ShapeDtypeStruct((n, D_MODEL), F32)
    x_spec = pl.BlockSpec((tm, D_MODEL), row)
    if router is None:
        return pl.pallas_call(
            _merge_kernel, grid=(n // tm,), in_specs=in_specs, out_specs=x_spec, out_shape=x_shape,
            compiler_params=pltpu.CompilerParams(dimension_semantics=("parallel",), vmem_limit_bytes=VMEM_LIMIT),
            name="merge",
        )(*args)
    g2, wr_hi, wr_lo = router
    tri = jnp.triu(jnp.ones((tm, tm), BF16), k=1)
    col = lambda i: (0, i)
    in_specs += [pl.BlockSpec((1, D_MODEL), const), pl.BlockSpec((D_MODEL, LANES), const),
                 pl.BlockSpec((D_MODEL, LANES), const), pl.BlockSpec((tm, tm), const)]
    return pl.pallas_call(
        _merge_route_kernel, grid=(n // tm,), in_specs=in_specs,
        out_specs=[x_spec, pl.BlockSpec((N_EXPERTS, tm), col), pl.BlockSpec((N_EXPERTS, tm), col),
                   pl.BlockSpec((N_EXPERTS, LANES), const)],
        out_shape=[x_shape, jax.ShapeDtypeStruct((N_EXPERTS, n), jnp.int32),
                   jax.ShapeDtypeStruct((N_EXPERTS, n), F32), jax.ShapeDtypeStruct((N_EXPERTS, LANES), F32)],
        scratch_shapes=[pltpu.VMEM((N_EXPERTS, LANES), F32)],
        compiler_params=pltpu.CompilerParams(dimension_semantics=("arbitrary",), vmem_limit_bytes=VMEM_LIMIT),
        name="merge_route",
    )(*args, g2, wr_hi, wr_lo, tri)


def _ffn_kernel(te_ref, tv_ref, x_ref, g_ref, wg_ref, wu_ref, wd_ref, o_ref, h_ref, acc_ref, *, residual):
    i = pl.program_id(0)
    f = pl.program_id(1)

    @pl.when(f == 0)
    def _():
        h_ref[...] = _rms(x_ref[...], g_ref[...]).astype(BF16)
        acc_ref[...] = jnp.zeros_like(acc_ref)

    @pl.when(tv_ref[i] == 1)
    def _():
        h = h_ref[...]
        gg = jnp.dot(h, wg_ref[...], preferred_element_type=F32)
        uu = jnp.dot(h, wu_ref[...], preferred_element_type=F32)
        a = (jax.nn.silu(gg) * uu).astype(BF16)
        acc_ref[...] += jnp.dot(a, wd_ref[...], preferred_element_type=F32)

    @pl.when(f == pl.num_programs(1) - 1)
    def _():
        o_ref[...] = (x_ref[...] + acc_ref[...]) if residual else acc_ref[...]


def _ffn(x, g, w_gu, w_down, tile_expert, tile_valid, *, residual, tm, tf):
    rows = x.shape[0]
    f_dim = w_down.shape[1]
    nf = f_dim // tf
    fe = lambda f, tv, i: f * tv[i] + (nf - 1) * (1 - tv[i])
    grid_spec = pltpu.PrefetchScalarGridSpec(
        num_scalar_prefetch=2,
        grid=(rows // tm, nf),
        in_specs=[pl.BlockSpec((tm, D_MODEL), lambda i, f, te, tv: (i, 0)),
                  pl.BlockSpec((1, D_MODEL), lambda i, f, te, tv: (0, 0)),
                  pl.BlockSpec((None, D_MODEL, tf), lambda i, f, te, tv: (te[i], 0, fe(f, tv, i))),
                  pl.BlockSpec((None, D_MODEL, tf), lambda i, f, te, tv: (te[i], 0, nf + fe(f, tv, i))),
                  pl.BlockSpec((None, tf, D_MODEL), lambda i, f, te, tv: (te[i], fe(f, tv, i), 0))],
        out_specs=pl.BlockSpec((tm, D_MODEL), lambda i, f, te, tv: (i, 0)),
        scratch_shapes=[pltpu.VMEM((tm, D_MODEL), BF16), pltpu.VMEM((tm, D_MODEL), F32)],
    )
    return pl.pallas_call(
        functools.partial(_ffn_kernel, residual=residual),
        grid_spec=grid_spec,
        out_shape=jax.ShapeDtypeStruct((rows, D_MODEL), F32),
        compiler_params=pltpu.CompilerParams(dimension_semantics=("parallel", "arbitrary"),
                                             vmem_limit_bytes=VMEM_LIMIT),
        name="ffn_res" if residual else "ffn_moe",
    )(tile_expert, tile_valid, x, g, w_gu, w_gu, w_down)


def _dispatch_kernel(pos_ref, x_ref, zeros_ref, xs_ref, sem):
    del zeros_ref
    td = x_ref.shape[0]

    def row_copy(t, slot):
        return pltpu.make_async_copy(x_ref.at[pl.ds(t, 1)], xs_ref.at[pl.ds(pos_ref[0, 2 * t + slot], 1)], sem)

    def start(t, carry):
        row_copy(t, 0).start()
        row_copy(t, 1).start()
        return carry

    def wait(t, carry):
        row_copy(t, 0).wait()
        row_copy(t, 1).wait()
        return carry

    lax.fori_loop(0, td, start, 0, unroll=ROW_DMA_UNROLL)
    lax.fori_loop(0, td, wait, 0, unroll=ROW_DMA_UNROLL)


def _dispatch(x, pos, rows_out, td=256):
    n = x.shape[0]
    pos3 = pos.reshape(n // td, 1, 2 * td)
    zeros = jnp.zeros((rows_out, D_MODEL), F32)
    return pl.pallas_call(
        _dispatch_kernel,
        grid=(n // td,),
        in_specs=[pl.BlockSpec((None, 1, 2 * td), lambda i: (i, 0, 0), memory_space=pltpu.SMEM),
                  pl.BlockSpec((td, D_MODEL), lambda i: (i, 0)),
                  pl.BlockSpec(memory_space=pl.ANY)],
        out_specs=pl.BlockSpec(memory_space=pl.ANY),
        out_shape=jax.ShapeDtypeStruct((rows_out, D_MODEL), F32),
        scratch_shapes=[pltpu.SemaphoreType.DMA(())],
        input_output_aliases={2: 0},
        compiler_params=pltpu.CompilerParams(dimension_semantics=("arbitrary",), vmem_limit_bytes=VMEM_LIMIT),
        name="moe_dispatch",
    )(pos3, x, zeros)


def _combine_kernel(pos_ref, x_ref, w_ref, ys_ref, o_ref, buf_ref, sem):
    tc = x_ref.shape[0]

    def row_copy(t, slot):
        return pltpu.make_async_copy(ys_ref.at[pl.ds(pos_ref[0, 2 * t + slot], 1)],
                                     buf_ref.at[slot, pl.ds(t, 1)], sem)

    def start(t, carry):
        row_copy(t, 0).start()
        row_copy(t, 1).start()
        return carry

    def wait(t, carry):
        row_copy(t, 0).wait()
        row_copy(t, 1).wait()
        return carry

    lax.fori_loop(0, tc, start, 0, unroll=ROW_DMA_UNROLL)
    lax.fori_loop(0, tc, wait, 0, unroll=ROW_DMA_UNROLL)
    w = w_ref[...]
    o_ref[...] = x_ref[...] + (w[:, 0:1] * buf_ref[0] + w[:, 1:2] * buf_ref[1])


def _combine(x, w, ys, pos, tc=256):
    n = x.shape[0]
    pos3 = pos.reshape(n // tc, 1, 2 * tc)
    return pl.pallas_call(
        _combine_kernel,
        grid=(n // tc,),
        in_specs=[pl.BlockSpec((None, 1, 2 * tc), lambda i: (i, 0, 0), memory_space=pltpu.SMEM),
                  pl.BlockSpec((tc, D_MODEL), lambda i: (i, 0)),
                  pl.BlockSpec((tc, LANES), lambda i: (i, 0)),
                  pl.BlockSpec(memory_space=pl.ANY)],
        out_specs=pl.BlockSpec((tc, D_MODEL), lambda i: (i, 0)),
        out_shape=jax.ShapeDtypeStruct((n, D_MODEL), F32),
        scratch_shapes=[pltpu.VMEM((2, tc, D_MODEL), F32), pltpu.SemaphoreType.DMA(())],
        compiler_params=pltpu.CompilerParams(dimension_semantics=("arbitrary",), vmem_limit_bytes=VMEM_LIMIT),
        name="moe_combine",
    )(pos3, x, w, ys)


def _moe(x, g2, route_i, route_w, counts, w_gu, w_down, tm=1024, tf=512):
    n = x.shape[0]
    n_tiles = (2 * n) // tm + N_EXPERTS
    cnt = counts[:, 0].astype(jnp.int32)
    padded = ((cnt + tm - 1) // tm) * tm
    off_end = jnp.cumsum(padded)
    off_start = off_end - padded
    experts = route_i[0:2].T
    pos = off_start[experts] + route_i[2:4].T
    weights = jnp.pad(route_w[0:2].T, ((0, 0), (0, LANES - 2)))
    tile_start = jnp.arange(n_tiles, dtype=jnp.int32) * tm
    tile_valid = (tile_start < off_end[-1]).astype(jnp.int32)
    tile_expert = jnp.minimum(jnp.sum(tile_start[:, None] >= off_end[None, :], axis=1), N_EXPERTS - 1)
    last_expert = tile_expert[jnp.maximum(off_end[-1] // tm - 1, 0)]
    tile_expert = jnp.where(tile_valid == 1, tile_expert, last_expert).astype(jnp.int32)
    xs = _dispatch(x, pos, n_tiles * tm)
    ys = _ffn(xs, g2, w_gu, w_down, tile_expert, tile_valid, residual=False, tm=tm, tf=tf)
    return _combine(x, weights, ys, pos)


def _rope_tables(seq_len):
    def angles(pos, dim):
        inv_freq = ROPE_THETA ** (-jnp.arange(0, dim, 2, dtype=F32) / dim)
        return pos.astype(F32)[:, None] * inv_freq[None, :]

    pos = jnp.arange(seq_len)
    a_seq = angles(pos, HEAD_DIM)
    a_row = angles(pos // GRID_W, HEAD_DIM // 2)
    a_col = angles(pos % GRID_W, HEAD_DIM // 2)
    cos_seq = jnp.concatenate([jnp.cos(a_seq)] * 2, axis=-1)
    sin_seq = jnp.concatenate([-jnp.sin(a_seq), jnp.sin(a_seq)], axis=-1)
    cos_ax = jnp.concatenate([jnp.cos(a_row)] * 2 + [jnp.cos(a_col)] * 2, axis=-1)
    sin_ax = jnp.concatenate([-jnp.sin(a_row), jnp.sin(a_row), -jnp.sin(a_col), jnp.sin(a_col)], axis=-1)
    two = lambda t: jnp.concatenate([t, t], axis=-1)
    return two(cos_ax), two(sin_ax), two(cos_seq), two(sin_seq)


def _qkv_column_order():
    cols = np.arange(QKV_W)
    qb = QA_W + np.concatenate([np.arange(h * HEAD_DIM, (h + 1) * HEAD_DIM) for h in GQA_HEAD_ORDER])
    cols[QA_W:QA_W + QB_W] = qb
    return cols


def _branch_row_order():
    rows = np.arange(QA_W + QB_W + LANES)
    rows[QA_W:QA_W + QB_W] = QA_W + np.concatenate(
        [np.arange(h * HEAD_DIM, (h + 1) * HEAD_DIM) for h in GQA_HEAD_ORDER])
    return rows


def kernel(x, norm_gains, w_in, qk_gains, na_rpb, w_branch, w_out, ffn_w_gu, ffn_w_down,
           moe_router, moe_w_gu, moe_w_down):
    batch, seq_len, d = x.shape
    depth = w_in.shape[0]
    n = batch * seq_len
    assert d == D_MODEL and seq_len == GRID_W * GRID_W

    tabs = _rope_tables(seq_len)
    na_bias = _na_bias_table(na_rpb)
    w_qkv = w_in[:, :, _qkv_column_order()].astype(BF16)
    w_gate = w_in[:, :, QKV_W:].astype(BF16)
    w_br = w_branch[:, _branch_row_order(), :].astype(BF16)
    w_o = w_out.astype(BF16)
    ffn_gu = ffn_w_gu.astype(BF16)
    ffn_dn = ffn_w_down.astype(BF16)
    moe_gu = moe_w_gu.astype(BF16)
    moe_dn = moe_w_down.astype(BF16)
    router = jnp.pad(moe_router, ((0, 0), (0, 0), (0, LANES - N_EXPERTS)))
    router_hi = router.astype(BF16)
    router_lo = (router - router_hi.astype(F32)).astype(BF16)
    gains = jnp.tile(qk_gains.reshape(depth, 6, HEAD_DIM), (1, 1, 2))
    q_scale = HEAD_DIM ** -0.5
    gains = gains * jnp.asarray([q_scale, 1.0, q_scale * LOG2_E, 1.0, q_scale, 1.0], F32)[None, :, None]
    gains = jnp.pad(gains, ((0, 0), (0, 2), (0, 0)))
    hd_idx = np.arange(2 * LANES) // HEAD_DIM
    block_ones = jnp.asarray(hd_idx[:, None] == hd_idx[None, :], BF16)
    one_tile = jnp.zeros((n // 1024,), jnp.int32)
    all_valid = jnp.ones((n // 1024,), jnp.int32)

    xf = x.reshape(n, d)
    for layer in range(depth):
        g1 = norm_gains[layer, 0][None, :]
        g2 = norm_gains[layer, 1][None, :]
        qa, ka, va, qb, kb, vb, *cs = _qkv_proj(xf, g1, w_qkv[layer], gains[layer], block_ones, tabs, seq_len)
        qc, kc, vc = cs[0:3], cs[3:6], cs[6:9]
        oa = _na_attention(qa, ka, va, na_bias[layer], batch, seq_len)
        ob = _gqa_attention(qb, kb, vb, batch, seq_len)
        dil = [_dil_attention(qc[j], kc[j], vc[j], DILATIONS[j], batch, seq_len) for j in range(3)]
        oc = [t[0] for t in dil]
        lse = [t[1] for t in dil]
        wb = w_br[layer]
        margs = (xf, g1, oa, ob, oc, lse, w_gate[layer], wb[:QA_W], wb[QA_W:QA_W + QB_W], wb[QA_W + QB_W:], w_o[layer])
        if layer % 2 == 0:
            xf = _merge(*margs)
            xf = _ffn(xf, g2, ffn_gu[layer // 2][None], ffn_dn[layer // 2][None], one_tile, all_valid,
                      residual=True, tm=1024, tf=256)
        else:
            i = layer // 2
            xf, route_i, route_w, counts = _merge(*margs, router=(g2, router_hi[i], router_lo[i]))
            xf = _moe(xf, g2, route_i, route_w, counts, moe_gu[i], moe_dn[i])
    return xf.reshape(batch, seq_len, d)
```

```python
import functools

import numpy as np
import jax
import jax.numpy as jnp
from jax import lax
from jax.experimental import pallas as pl
from jax.experimental.pallas import tpu as pltpu

D_MODEL = 1024
HEAD_DIM = 64
GRID_W = 64
EPS = 1e-6
ROPE_THETA = 10000.0
NEG_INF = -1e30
LOG2_E = 1.4426950408889634
NA_HEADS = 4
NA_WIN_ROWS = 8
NA_WIN_COLS = 16
GQA_Q_HEADS = 6
GQA_KV_HEADS = 2
DILATIONS = (1, 4, 16)
DIL_SIDE = 64
QA_W, QB_W, QC_W = 256, 384, 384
KA_W, KB_W, KC_W = 256, 128, 384
Q_W = QA_W + QB_W + QC_W
KV_W = KA_W + KB_W + KC_W
QK_W = Q_W + KV_W
QKV_W = Q_W + 2 * KV_W
GATE_W = 3 * D_MODEL
N_EXPERTS = 8
LANES = 128
VMEM_LIMIT = 48 * 1024 * 1024
NA_UNROLL = 8
ROW_DMA_UNROLL = 8

GQA_HEAD_ORDER = (0, 3, 1, 4, 2, 5)

F32 = jnp.float32
BF16 = jnp.bfloat16


def _nt_dot(a, b):
    return lax.dot_general(a, b, (((1,), (1,)), ((), ())), preferred_element_type=F32)


def _rms(x, g):
    ms = jnp.mean(x * x, axis=-1, keepdims=True)
    return x * lax.rsqrt(ms + EPS) * g


_QK_CHUNKS = (
    (0, None, ("qa", 0)), (0, None, ("qa", 1)),
    (2, "axial", ("qb", 0)), (2, "axial", ("qb", 1)), (2, "axial", ("qb", 2)),
    (4, "seq", ("qc", 0)), (4, "seq", ("qc", 1)), (4, "seq", ("qc", 2)),
    (1, None, ("ka", 0)), (1, None, ("ka", 1)),
    (3, "axial", ("kb", 0)),
    (5, "seq", ("kc", 0)), (5, "seq", ("kc", 1)), (5, "seq", ("kc", 2)),
)


def _qkv_kernel(x_ref, g_ref, w_ref, gains_ref, bd_ref, cax_ref, sax_ref, cseq_ref, sseq_ref,
                qa_ref, ka_ref, va_ref, qb_ref, kb_ref, vb_ref, *c_refs):
    qc_refs, kc_refs, vc_refs = c_refs[0:3], c_refs[3:6], c_refs[6:9]
    tm = x_ref.shape[0]
    h = _rms(x_ref[...], g_ref[...]).astype(BF16)
    p = jnp.dot(h, w_ref[...], preferred_element_type=F32)
    lane = lax.broadcasted_iota(jnp.int32, (tm, LANES), 1)
    dests = {"qa": qa_ref, "ka": ka_ref, "qb": qb_ref, "kb": kb_ref, "qc": qc_refs, "kc": kc_refs}
    sumsq = None
    for c, (grow, rope, (dname, dj)) in enumerate(_QK_CHUNKS):
        if c % 2 == 0:
            pc2 = p[:, c * LANES:(c + 2) * LANES]
            sumsq = jnp.dot((pc2 * pc2).astype(BF16), bd_ref[...], preferred_element_type=F32)
        pc = p[:, c * LANES:(c + 1) * LANES]
        sc = sumsq[:, (c % 2) * LANES:(c % 2 + 1) * LANES]
        y = pc * lax.rsqrt(sc * (1.0 / HEAD_DIM) + EPS) * gains_ref[grow:grow + 1, :]
        if rope is not None:
            half = 16 if rope == "axial" else 32
            cos_t = cax_ref[...] if rope == "axial" else cseq_ref[...]
            sin_t = sax_ref[...] if rope == "axial" else sseq_ref[...]
            partner = jnp.where((lane % (2 * half)) < half,
                                pltpu.roll(y, LANES - half, axis=1), pltpu.roll(y, half, axis=1))
            y = y * cos_t + partner * sin_t
        yb = y.astype(BF16)
        d = dests[dname]
        if dname in ("qc", "kc"):
            d[dj][...] = yb
        else:
            d[:, dj * LANES:(dj + 1) * LANES] = yb
    v = p[:, QK_W:].astype(BF16)
    va_ref[...] = v[:, :KA_W]
    vb_ref[...] = v[:, KA_W:KA_W + KB_W]
    for j in range(3):
        vc_refs[j][...] = v[:, KA_W + KB_W + j * LANES:KA_W + KB_W + (j + 1) * LANES]


def _qkv_proj(x, g, w, gains, bd, tabs, seq_len, tm=512):
    n = x.shape[0]
    nb = seq_len // tm
    row = lambda i: (i, 0)
    const = lambda i: (0, 0)
    tab = lambda i: (i % nb, 0)
    widths = (QA_W, KA_W, KA_W, QB_W, KB_W, KB_W) + (LANES,) * 9
    out_shapes = [jax.ShapeDtypeStruct((n, w_), BF16) for w_ in widths]
    out_specs = [pl.BlockSpec((tm, w_), row) for w_ in widths]
    return pl.pallas_call(
        _qkv_kernel,
        grid=(n // tm,),
        in_specs=[pl.BlockSpec((tm, D_MODEL), row), pl.BlockSpec((1, D_MODEL), const),
                  pl.BlockSpec((D_MODEL, QKV_W), const), pl.BlockSpec((8, LANES), const),
                  pl.BlockSpec((2 * LANES, 2 * LANES), const)] + [pl.BlockSpec((tm, LANES), tab)] * 4,
        out_specs=out_specs,
        out_shape=out_shapes,
        compiler_params=pltpu.CompilerParams(dimension_semantics=("parallel",), vmem_limit_bytes=VMEM_LIMIT),
        name="qkv_proj",
    )(x, g, w, gains, bd, *tabs)


def _na_kernel(q_ref, k_ref, v_ref, bias_ref, o_ref, *, rows_per_step):
    rg = pl.program_id(1)
    n_rows = k_ref.shape[0] // GRID_W
    nk = NA_WIN_ROWS * GRID_W
    lo = lax.broadcasted_iota(jnp.int32, (GRID_W, LANES), 1) < HEAD_DIM

    def body(rr, carry):
        r = rg * rows_per_step + rr
        row_start = jnp.clip(r - NA_WIN_ROWS // 2, 0, n_rows - NA_WIN_ROWS)
        off = r - row_start
        ks = pl.multiple_of(row_start * GRID_W, GRID_W)
        qs = pl.multiple_of(rr * GRID_W, GRID_W)
        qrow = q_ref[pl.ds(qs, GRID_W), :]
        outs = []
        for p in range(NA_HEADS // 2):
            q2 = qrow[:, p * LANES:(p + 1) * LANES]
            z = jnp.zeros_like(q2)
            lhs = jnp.concatenate([jnp.where(lo, q2, z), jnp.where(lo, z, q2)], axis=0)
            kblk = k_ref[pl.ds(ks, nk), p * LANES:(p + 1) * LANES]
            vblk = v_ref[pl.ds(ks, nk), p * LANES:(p + 1) * LANES]
            s = _nt_dot(lhs, kblk)
            s = s + jnp.concatenate([bias_ref[2 * p, off], bias_ref[2 * p + 1, off]], axis=0)
            m = jnp.max(s, axis=-1, keepdims=True)
            e = jnp.exp(s - m)
            l = jnp.sum(e, axis=-1, keepdims=True)
            pv = jnp.dot(e.astype(BF16), vblk, preferred_element_type=F32) / l
            outs.append(jnp.where(lo, pv[:GRID_W], pv[GRID_W:]))
        o_ref[pl.ds(qs, GRID_W), :] = jnp.concatenate(outs, axis=1).astype(o_ref.dtype)
        return carry

    lax.fori_loop(0, rows_per_step, body, 0, unroll=NA_UNROLL)


def _na_bias_table(rpb):
    kw = NA_WIN_COLS
    c = np.arange(GRID_W)
    col_start = np.clip(c - kw // 2, 0, GRID_W - kw)
    col_mask = (c[None, :] >= col_start[:, None]) & (c[None, :] < col_start[:, None] + kw)
    dc = np.clip(c[None, :] - c[:, None], -(kw - 1), kw - 1) + NA_WIN_COLS - 1
    onehot = jnp.asarray(dc[None, :, :] == np.arange(2 * kw - 1)[:, None, None], F32)
    by_col = jnp.einsum("...rc,cqk->...rqk", rpb.astype(F32), onehot, precision=lax.Precision.HIGHEST)
    top = NA_WIN_ROWS - 1
    b = jnp.stack([by_col[..., top - off:top - off + NA_WIN_ROWS, :, :] for off in range(NA_WIN_ROWS)], axis=-4)
    b = jnp.swapaxes(b, -3, -2)
    b = jnp.where(col_mask[:, None, :], b, NEG_INF)
    return b.reshape(*rpb.shape[:-2], NA_WIN_ROWS, GRID_W, NA_WIN_ROWS * GRID_W)


def _na_attention(q, k, v, bias, batch, seq_len, rows_per_step=8):
    q3, k3, v3 = (t.reshape(batch, seq_len, QA_W) for t in (q, k, v))
    tq = rows_per_step * GRID_W
    out = pl.pallas_call(
        functools.partial(_na_kernel, rows_per_step=rows_per_step),
        grid=(batch, seq_len // tq),
        in_specs=[pl.BlockSpec((None, tq, QA_W), lambda b, i: (b, i, 0)),
                  pl.BlockSpec((None, seq_len, QA_W), lambda b, i: (b, 0, 0)),
                  pl.BlockSpec((None, seq_len, QA_W), lambda b, i: (b, 0, 0)),
                  pl.BlockSpec(bias.shape, lambda b, i: (0, 0, 0, 0))],
        out_specs=pl.BlockSpec((None, tq, QA_W), lambda b, i: (b, i, 0)),
        out_shape=jax.ShapeDtypeStruct((batch, seq_len, QA_W), BF16),
        compiler_params=pltpu.CompilerParams(dimension_semantics=("parallel", "parallel"),
                                             vmem_limit_bytes=VMEM_LIMIT),
        name="na_attn",
    )(q3, k3, v3, bias)
    return out.reshape(batch * seq_len, QA_W)


def _gqa_kernel(q_ref, k_ref, v_ref, o_ref, *, tk):
    tq = q_ref.shape[0]
    n_pairs = QB_W // LANES
    lo = lax.broadcasted_iota(jnp.int32, (tq, LANES), 1) < HEAD_DIM
    q = q_ref[...]
    z = jnp.zeros((tq, LANES), q.dtype)
    pairs = [q[:, p * LANES:(p + 1) * LANES] for p in range(n_pairs)]
    lhs = jnp.concatenate([jnp.where(lo, t, z) for t in pairs] + [jnp.where(lo, z, t) for t in pairs], axis=0)
    rows = lhs.shape[0]

    def body(kc, carry):
        m, l, acc = carry
        ks = pl.multiple_of(kc * tk, tk)
        kblk = k_ref[pl.ds(ks, tk), :]
        vblk = v_ref[pl.ds(ks, tk), :]
        s = _nt_dot(lhs, kblk)
        m_new = jnp.maximum(m, jnp.max(s, axis=-1, keepdims=True))
        e = jnp.exp2(s - m_new)
        alpha = jnp.exp2(m - m_new)
        l = alpha * l + jnp.sum(e, axis=-1, keepdims=True)
        acc = alpha * acc + jnp.dot(e.astype(BF16), vblk, preferred_element_type=F32)
        return m_new, l, acc

    init = (jnp.full((rows, 1), NEG_INF, F32), jnp.zeros((rows, 1), F32), jnp.zeros((rows, LANES), F32))
    _, l, acc = lax.fori_loop(0, k_ref.shape[0] // tk, body, init)
    o = acc / l
    half = n_pairs * tq
    for p in range(n_pairs):
        o_ref[:, p * LANES:(p + 1) * LANES] = jnp.where(
            lo, o[p * tq:(p + 1) * tq], o[half + p * tq:half + (p + 1) * tq]).astype(o_ref.dtype)


def _gqa_attention(q, k, v, batch, seq_len, tq=128, tk=4096):
    q3 = q.reshape(batch, seq_len, QB_W)
    k3 = k.reshape(batch, seq_len, KB_W)
    v3 = v.reshape(batch, seq_len, KB_W)
    out = pl.pallas_call(
        functools.partial(_gqa_kernel, tk=tk),
        grid=(batch, seq_len // tq),
        in_specs=[pl.BlockSpec((None, tq, QB_W), lambda b, i: (b, i, 0)),
                  pl.BlockSpec((None, seq_len, KB_W), lambda b, i: (b, 0, 0)),
                  pl.BlockSpec((None, seq_len, KB_W), lambda b, i: (b, 0, 0))],
        out_specs=pl.BlockSpec((None, tq, QB_W), lambda b, i: (b, i, 0)),
        out_shape=jax.ShapeDtypeStruct((batch, seq_len, QB_W), BF16),
        compiler_params=pltpu.CompilerParams(dimension_semantics=("parallel", "parallel"),
                                             vmem_limit_bytes=VMEM_LIMIT),
        name="gqa_attn",
    )(q3, k3, v3)
    return out.reshape(batch * seq_len, QB_W)


def _dil_kernel(q_ref, k_ref, v_ref, o_ref, lse_ref):
    sub_len = q_ref.shape[0]
    tq = 2 * DIL_SIDE
    tkk = 4 * DIL_SIDE
    lo = lax.broadcasted_iota(jnp.int32, (tq, LANES), 1) < HEAD_DIM
    qi_local = lax.broadcasted_iota(jnp.int32, (2 * tq, tkk), 0) % tq
    ki_local = lax.broadcasted_iota(jnp.int32, (2 * tq, tkk), 1)

    def body(qb, carry):
        i0 = pl.multiple_of(qb * tq, tq)
        ks = pl.multiple_of(jnp.clip(i0 - DIL_SIDE, 0, sub_len - tkk), DIL_SIDE)
        q2 = q_ref[pl.ds(i0, tq), :]
        z = jnp.zeros_like(q2)
        lhs = jnp.concatenate([jnp.where(lo, q2, z), jnp.where(lo, z, q2)], axis=0)
        s = _nt_dot(lhs, k_ref[pl.ds(ks, tkk), :])
        delta = (qi_local + i0) - (ki_local + ks)
        s = jnp.where((delta <= DIL_SIDE) & (delta >= -DIL_SIDE), s, NEG_INF)
        m = jnp.max(s, axis=-1, keepdims=True)
        e = jnp.exp(s - m)
        l = jnp.sum(e, axis=-1, keepdims=True)
        pv = jnp.dot(e.astype(BF16), v_ref[pl.ds(ks, tkk), :], preferred_element_type=F32) / l
        lse = m + jnp.log(l)
        o_ref[pl.ds(i0, tq), :] = jnp.where(lo, pv[:tq], pv[tq:])
        lse_ref[pl.ds(i0, tq), :] = jnp.where(lo, lse[:tq], lse[tq:])
        return carry

    lax.fori_loop(0, sub_len // tq, body, 0, unroll=min(4, sub_len // tq))


def _dil_attention(q, k, v, dil, batch, seq_len):
    sub_len = seq_len // dil
    view = lambda t: t.reshape(batch, sub_len, dil * LANES)
    spec = pl.BlockSpec((None, sub_len, LANES), lambda b, r: (b, 0, r))
    o, lse = pl.pallas_call(
        _dil_kernel,
        grid=(batch, dil),
        in_specs=[spec, spec, spec],
        out_specs=[spec, spec],
        out_shape=[jax.ShapeDtypeStruct((batch, sub_len, dil * LANES), F32)] * 2,
        compiler_params=pltpu.CompilerParams(dimension_semantics=("parallel", "parallel"),
                                             vmem_limit_bytes=VMEM_LIMIT),
        name=f"dil_attn_d{dil}",
    )(view(q), view(k), view(v))
    n = batch * seq_len
    return o.reshape(n, LANES), lse.reshape(n, LANES)


def _merge_body(x_ref, g_ref, oa_ref, ob_ref, oc0_ref, oc1_ref, oc2_ref, lse0_ref, lse1_ref, lse2_ref,
                wg_ref, wba_ref, wbb_ref, wbc_ref, wo_ref):
    x = x_ref[...]
    h = _rms(x, g_ref[...]).astype(BF16)
    lse = [lse0_ref[...], lse1_ref[...], lse2_ref[...]]
    mx = jnp.maximum(jnp.maximum(lse[0], lse[1]), lse[2])
    ex = [jnp.exp(t - mx) for t in lse]
    den = ex[0] + ex[1] + ex[2]
    oc = (ex[0] / den) * oc0_ref[...] + (ex[1] / den) * oc1_ref[...] + (ex[2] / den) * oc2_ref[...]
    ya = jnp.dot(oa_ref[...], wba_ref[...], preferred_element_type=F32)
    yb = jnp.dot(ob_ref[...], wbb_ref[...], preferred_element_type=F32)
    yc = jnp.dot(oc.astype(BF16), wbc_ref[...], preferred_element_type=F32)
    merged = None
    for j, y in enumerate((ya, yb, yc)):
        gate = jnp.dot(h, wg_ref[:, j * D_MODEL:(j + 1) * D_MODEL], preferred_element_type=F32)
        t = jax.nn.sigmoid(gate) * y
        merged = t if merged is None else merged + t
    return x + jnp.dot(merged.astype(BF16), wo_ref[...], preferred_element_type=F32)


def _merge_kernel(*refs):
    *ins, out_ref = refs
    out_ref[...] = _merge_body(*ins)


def _merge_route_kernel(*refs):
    *ins, g2_ref, wrh_ref, wrl_ref, tri_ref, out_ref, ri_ref, rw_ref, cnt_ref, carry_ref = refs
    xn = _merge_body(*ins)
    out_ref[...] = xn
    tm = xn.shape[0]

    @pl.when(pl.program_id(0) == 0)
    def _():
        carry_ref[...] = jnp.zeros_like(carry_ref)

    h2 = _rms(xn, g2_ref[...])
    h_hi = h2.astype(BF16)
    h_lo = (h2 - h_hi.astype(F32)).astype(BF16)
    logits = (jnp.dot(h_hi, wrh_ref[...], preferred_element_type=F32)
              + jnp.dot(h_lo, wrh_ref[...], preferred_element_type=F32)
              + jnp.dot(h_hi, wrl_ref[...], preferred_element_type=F32))
    lt = logits.T[:N_EXPERTS]
    row = lax.broadcasted_iota(jnp.int32, (N_EXPERTS, tm), 0)
    row_f = row.astype(F32)
    v1 = jnp.max(lt, axis=0, keepdims=True)
    i1 = jnp.min(jnp.where(lt == v1, row_f, float(N_EXPERTS)), axis=0, keepdims=True)
    hot1 = row_f == i1
    rest = jnp.where(hot1, -jnp.inf, lt)
    v2 = jnp.max(rest, axis=0, keepdims=True)
    i2 = jnp.min(jnp.where(rest == v2, row_f, float(N_EXPERTS)), axis=0, keepdims=True)
    hot2 = row_f == i2
    e2 = jnp.exp(v2 - v1)
    w1 = 1.0 / (1.0 + e2)
    w2 = e2 / (1.0 + e2)
    hot = jnp.where(hot1 | hot2, 1.0, 0.0)
    hot16 = jnp.concatenate([hot, jnp.zeros_like(hot)], axis=0).astype(BF16)
    carry = carry_ref[:, 0:1]
    before = jnp.dot(hot16, tri_ref[...], preferred_element_type=F32)[:N_EXPERTS] + carry
    r1 = jnp.sum(jnp.where(hot1, before, 0.0), axis=0, keepdims=True)
    r2 = jnp.sum(jnp.where(hot2, before, 0.0), axis=0, keepdims=True)
    total = carry + jnp.sum(hot, axis=1, keepdims=True)
    carry_ref[...] = jnp.broadcast_to(total, carry_ref.shape)
    cnt_ref[...] = jnp.broadcast_to(total, cnt_ref.shape)
    ri = jnp.where(row == 0, i1, jnp.where(row == 1, i2, jnp.where(row == 2, r1, r2)))
    ri_ref[...] = ri.astype(jnp.int32)
    rw_ref[...] = jnp.where(row == 0, w1, jnp.where(row == 1, w2, 0.0))


def _merge(x, g1, oa, ob, oc, lse, wg, wba, wbb, wbc, wo, router=None, tm=512):
    n = x.shape[0]
    row = lambda i: (i, 0)
    const = lambda i: (0, 0)
    in_specs = [pl.BlockSpec((tm, D_MODEL), row), pl.BlockSpec((1, D_MODEL), const),
                pl.BlockSpec((tm, QA_W), row), pl.BlockSpec((tm, QB_W), row)]
    in_specs += [pl.BlockSpec((tm, LANES), row)] * 6
    in_specs += [pl.BlockSpec((D_MODEL, GATE_W), const), pl.BlockSpec((QA_W, D_MODEL), const),
                 pl.BlockSpec((QB_W, D_MODEL), const), pl.BlockSpec((LANES, D_MODEL), const),
                 pl.BlockSpec((D_MODEL, D_MODEL), const)]
    args = [x, g1, oa, ob, *oc, *lse, wg, wba, wbb, wbc, wo]
    x_shape = jax.ShapeDtypeStruct((n, D_MODEL), F32)
    x_spec = pl.BlockSpec((tm, D_MODEL), row)
    if router is None:
        return pl.pallas_call(
            _merge_kernel, grid=(n // tm,), in_specs=in_specs, out_specs=x_spec, out_shape=x_shape,
            compiler_params=pltpu.CompilerParams(dimension_semantics=("parallel",), vmem_limit_bytes=VMEM_LIMIT),
            name="merge",
        )(*args)
    g2, wr_hi, wr_lo = router
    tri = jnp.triu(jnp.ones((tm, tm), BF16), k=1)
    col = lambda i: (0, i)
    in_specs += [pl.BlockSpec((1, D_MODEL), const), pl.BlockSpec((D_MODEL, LANES), const),
                 pl.BlockSpec((D_MODEL, LANES), const), pl.BlockSpec((tm, tm), const)]
    return pl.pallas_call(
        _merge_route_kernel, grid=(n // tm,), in_specs=in_specs,
        out_specs=[x_spec, pl.BlockSpec((N_EXPERTS, tm), col), pl.BlockSpec((N_EXPERTS, tm), col),
                   pl.BlockSpec((N_EXPERTS, LANES), const)],
        out_shape=[x_shape, jax.ShapeDtypeStruct((N_EXPERTS, n), jnp.int32),
                   jax.ShapeDtypeStruct((N_EXPERTS, n), F32), jax.ShapeDtypeStruct((N_EXPERTS, LANES), F32)],
        scratch_shapes=[pltpu.VMEM((N_EXPERTS, LANES), F32)],
        compiler_params=pltpu.CompilerParams(dimension_semantics=("arbitrary",), vmem_limit_bytes=VMEM_LIMIT),
        name="merge_route",
    )(*args, g2, wr_hi, wr_lo, tri)


def _ffn_kernel(te_ref, tv_ref, x_ref, g_ref, wg_ref, wu_ref, wd_ref, o_ref, h_ref, acc_ref, *, residual):
    i = pl.program_id(0)
    f = pl.program_id(1)

    @pl.when(f == 0)
    def _():
        h_ref[...] = _rms(x_ref[...], g_ref[...]).astype(BF16)
        acc_ref[...] = jnp.zeros_like(acc_ref)

    @pl.when(tv_ref[i] == 1)
    def _():
        h = h_ref[...]
        gg = jnp.dot(h, wg_ref[...].astype(BF16), preferred_element_type=F32)
        uu = jnp.dot(h, wu_ref[...].astype(BF16), preferred_element_type=F32)
        a = (jax.nn.silu(gg) * uu).astype(BF16)
        acc_ref[...] += jnp.dot(a, wd_ref[...].astype(BF16), preferred_element_type=F32)

    @pl.when(f == pl.num_programs(1) - 1)
    def _():
        o_ref[...] = (x_ref[...] + acc_ref[...]) if residual else acc_ref[...]


def _ffn(x, g, w_gu, w_down, tile_expert, tile_valid, *, residual, tm, tf):
    rows = x.shape[0]
    f_dim = w_down.shape[1]
    nf = f_dim // tf
    fe = lambda f, tv, i: f * tv[i] + (nf - 1) * (1 - tv[i])
    grid_spec = pltpu.PrefetchScalarGridSpec(
        num_scalar_prefetch=2,
        grid=(rows // tm, nf),
        in_specs=[pl.BlockSpec((tm, D_MODEL), lambda i, f, te, tv: (i, 0)),
                  pl.BlockSpec((1, D_MODEL), lambda i, f, te, tv: (0, 0)),
                  pl.BlockSpec((None, D_MODEL, tf), lambda i, f, te, tv: (te[i], 0, fe(f, tv, i))),
                  pl.BlockSpec((None, D_MODEL, tf), lambda i, f, te, tv: (te[i], 0, nf + fe(f, tv, i))),
                  pl.BlockSpec((None, tf, D_MODEL), lambda i, f, te, tv: (te[i], fe(f, tv, i), 0))],
        out_specs=pl.BlockSpec((tm, D_MODEL), lambda i, f, te, tv: (i, 0)),
        scratch_shapes=[pltpu.VMEM((tm, D_MODEL), BF16), pltpu.VMEM((tm, D_MODEL), F32)],
    )
    return pl.pallas_call(
        functools.partial(_ffn_kernel, residual=residual),
        grid_spec=grid_spec,
        out_shape=jax.ShapeDtypeStruct((rows, D_MODEL), F32),
        compiler_params=pltpu.CompilerParams(dimension_semantics=("parallel", "arbitrary"),
                                             vmem_limit_bytes=VMEM_LIMIT),
        name="ffn_res" if residual else "ffn_moe",
    )(tile_expert, tile_valid, x, g, w_gu, w_gu, w_down)


def _dispatch_kernel(pos_ref, x_ref, zeros_ref, xs_ref, sem):
    del zeros_ref
    td = x_ref.shape[0]

    def row_copy(t, slot):
        return pltpu.make_async_copy(x_ref.at[pl.ds(t, 1)], xs_ref.at[pl.ds(pos_ref[0, 2 * t + slot], 1)], sem)

    def start(t, carry):
        row_copy(t, 0).start()
        row_copy(t, 1).start()
        return carry

    def wait(t, carry):
        row_copy(t, 0).wait()
        row_copy(t, 1).wait()
        return carry

    lax.fori_loop(0, td, start, 0, unroll=ROW_DMA_UNROLL)
    lax.fori_loop(0, td, wait, 0, unroll=ROW_DMA_UNROLL)


def _dispatch(x, pos, rows_out, td=256):
    n = x.shape[0]
    pos3 = pos.reshape(n // td, 1, 2 * td)
    zeros = jnp.zeros((rows_out, D_MODEL), F32)
    return pl.pallas_call(
        _dispatch_kernel,
        grid=(n // td,),
        in_specs=[pl.BlockSpec((None, 1, 2 * td), lambda i: (i, 0, 0), memory_space=pltpu.SMEM),
                  pl.BlockSpec((td, D_MODEL), lambda i: (i, 0)),
                  pl.BlockSpec(memory_space=pl.ANY)],
        out_specs=pl.BlockSpec(memory_space=pl.ANY),
        out_shape=jax.ShapeDtypeStruct((rows_out, D_MODEL), F32),
        scratch_shapes=[pltpu.SemaphoreType.DMA(())],
        input_output_aliases={2: 0},
        compiler_params=pltpu.CompilerParams(dimension_semantics=("arbitrary",), vmem_limit_bytes=VMEM_LIMIT),
        name="moe_dispatch",
    )(pos3, x, zeros)


def _combine_kernel(pos_ref, x_ref, w_ref, ys_ref, o_ref, buf_ref, sem):
    tc = x_ref.shape[0]

    def row_copy(t, slot):
        return pltpu.make_async_copy(ys_ref.at[pl.ds(pos_ref[0, 2 * t + slot], 1)],
                                     buf_ref.at[slot, pl.ds(t, 1)], sem)

    def start(t, carry):
        row_copy(t, 0).start()
        row_copy(t, 1).start()
        return carry

    def wait(t, carry):
        row_copy(t, 0).wait()
        row_copy(t, 1).wait()
        return carry

    lax.fori_loop(0, tc, start, 0, unroll=ROW_DMA_UNROLL)
    lax.fori_loop(0, tc, wait, 0, unroll=ROW_DMA_UNROLL)
    w = w_ref[...]
    o_ref[...] = x_ref[...] + (w[:, 0:1] * buf_ref[0] + w[:, 1:2] * buf_ref[1])


def _combine(x, w, ys, pos, tc=256):
    n = x.shape[0]
    pos3 = pos.reshape(n // tc, 1, 2 * tc)
    return pl.pallas_call(
        _combine_kernel,
        grid=(n // tc,),
        in_specs=[pl.BlockSpec((None, 1, 2 * tc), lambda i: (i, 0, 0), memory_space=pltpu.SMEM),
                  pl.BlockSpec((tc, D_MODEL), lambda i: (i, 0)),
                  pl.BlockSpec((tc, LANES), lambda i: (i, 0)),
                  pl.BlockSpec(memory_space=pl.ANY)],
        out_specs=pl.BlockSpec((tc, D_MODEL), lambda i: (i, 0)),
        out_shape=jax.ShapeDtypeStruct((n, D_MODEL), F32),
        scratch_shapes=[pltpu.VMEM((2, tc, D_MODEL), F32), pltpu.SemaphoreType.DMA(())],
        compiler_params=pltpu.CompilerParams(dimension_semantics=("arbitrary",), vmem_limit_bytes=VMEM_LIMIT),
        name="moe_combine",
    )(pos3, x, w, ys)


def _moe(x, g2, route_i, route_w, counts, w_gu, w_down, tm=1024, tf=512):
    n = x.shape[0]
    n_tiles = (2 * n) // tm + N_EXPERTS
    cnt = counts[:, 0].astype(jnp.int32)
    padded = ((cnt + tm - 1) // tm) * tm
    off_end = jnp.cumsum(padded)
    off_start = off_end - padded
    experts = route_i[0:2].T
    pos = off_start[experts] + route_i[2:4].T
    weights = jnp.pad(route_w[0:2].T, ((0, 0), (0, LANES - 2)))
    tile_start = jnp.arange(n_tiles, dtype=jnp.int32) * tm
    tile_valid = (tile_start < off_end[-1]).astype(jnp.int32)
    tile_expert = jnp.minimum(jnp.sum(tile_start[:, None] >= off_end[None, :], axis=1), N_EXPERTS - 1)
    last_expert = tile_expert[jnp.maximum(off_end[-1] // tm - 1, 0)]
    tile_expert = jnp.where(tile_valid == 1, tile_expert, last_expert).astype(jnp.int32)
    xs = _dispatch(x, pos, n_tiles * tm)
    ys = _ffn(xs, g2, w_gu, w_down, tile_expert, tile_valid, residual=False, tm=tm, tf=tf)
    return _combine(x, weights, ys, pos)


def _rope_tables(seq_len):
    def angles(pos, dim):
        inv_freq = ROPE_THETA ** (-jnp.arange(0, dim, 2, dtype=F32) / dim)
        return pos.astype(F32)[:, None] * inv_freq[None, :]

    pos = jnp.arange(seq_len)
    a_seq = angles(pos, HEAD_DIM)
    a_row = angles(pos // GRID_W, HEAD_DIM // 2)
    a_col = angles(pos % GRID_W, HEAD_DIM // 2)
    cos_seq = jnp.concatenate([jnp.cos(a_seq)] * 2, axis=-1)
    sin_seq = jnp.concatenate([-jnp.sin(a_seq), jnp.sin(a_seq)], axis=-1)
    cos_ax = jnp.concatenate([jnp.cos(a_row)] * 2 + [jnp.cos(a_col)] * 2, axis=-1)
    sin_ax = jnp.concatenate([-jnp.sin(a_row), jnp.sin(a_row), -jnp.sin(a_col), jnp.sin(a_col)], axis=-1)
    two = lambda t: jnp.concatenate([t, t], axis=-1)
    return two(cos_ax), two(sin_ax), two(cos_seq), two(sin_seq)


def _qkv_column_order():
    cols = np.arange(QKV_W)
    qb = QA_W + np.concatenate([np.arange(h * HEAD_DIM, (h + 1) * HEAD_DIM) for h in GQA_HEAD_ORDER])
    cols[QA_W:QA_W + QB_W] = qb
    return cols


def _branch_row_order():
    rows = np.arange(QA_W + QB_W + LANES)
    rows[QA_W:QA_W + QB_W] = QA_W + np.concatenate(
        [np.arange(h * HEAD_DIM, (h + 1) * HEAD_DIM) for h in GQA_HEAD_ORDER])
    return rows


def kernel(x, norm_gains, w_in, qk_gains, na_rpb, w_branch, w_out, ffn_w_gu, ffn_w_down,
           moe_router, moe_w_gu, moe_w_down):
    batch, seq_len, d = x.shape
    depth = w_in.shape[0]
    n = batch * seq_len
    assert d == D_MODEL and seq_len == GRID_W * GRID_W

    tabs = _rope_tables(seq_len)
    na_bias = _na_bias_table(na_rpb)
    w_qkv = w_in[:, :, _qkv_column_order()].astype(BF16)
    w_gate = w_in[:, :, QKV_W:].astype(BF16)
    w_br = w_branch[:, _branch_row_order(), :].astype(BF16)
    w_o = w_out.astype(BF16)
    ffn_gu = ffn_w_gu.astype(BF16)
    ffn_dn = ffn_w_down.astype(BF16)
    moe_gu = moe_w_gu
    moe_dn = moe_w_down
    router = jnp.pad(moe_router, ((0, 0), (0, 0), (0, LANES - N_EXPERTS)))
    router_hi = router.astype(BF16)
    router_lo = (router - router_hi.astype(F32)).astype(BF16)
    gains = jnp.tile(qk_gains.reshape(depth, 6, HEAD_DIM), (1, 1, 2))
    q_scale = HEAD_DIM ** -0.5
    gains = gains * jnp.asarray([q_scale, 1.0, q_scale * LOG2_E, 1.0, q_scale, 1.0], F32)[None, :, None]
    gains = jnp.pad(gains, ((0, 0), (0, 2), (0, 0)))
    hd_idx = np.arange(2 * LANES) // HEAD_DIM
    block_ones = jnp.asarray(hd_idx[:, None] == hd_idx[None, :], BF16)
    one_tile = jnp.zeros((n // 1024,), jnp.int32)
    all_valid = jnp.ones((n // 1024,), jnp.int32)

    xf = x.reshape(n, d)
    for layer in range(depth):
        g1 = norm_gains[layer, 0][None, :]
        g2 = norm_gains[layer, 1][None, :]
        qa, ka, va, qb, kb, vb, *cs = _qkv_proj(xf, g1, w_qkv[layer], gains[layer], block_ones, tabs, seq_len)
        qc, kc, vc = cs[0:3], cs[3:6], cs[6:9]
        oa = _na_attention(qa, ka, va, na_bias[layer], batch, seq_len)
        ob = _gqa_attention(qb, kb, vb, batch, seq_len)
        dil = [_dil_attention(qc[j], kc[j], vc[j], DILATIONS[j], batch, seq_len) for j in range(3)]
        oc = [t[0] for t in dil]
        lse = [t[1] for t in dil]
        wb = w_br[layer]
        margs = (xf, g1, oa, ob, oc, lse, w_gate[layer], wb[:QA_W], wb[QA_W:QA_W + QB_W], wb[QA_W + QB_W:], w_o[layer])
        if layer % 2 == 0:
            xf = _merge(*margs)
            xf = _ffn(xf, g2, ffn_gu[layer // 2][None], ffn_dn[layer // 2][None], one_tile, all_valid,
                      residual=True, tm=1024, tf=256)
        else:
            i = layer // 2
            xf, route_i, route_w, counts = _merge(*margs, router=(g2, router_hi[i], router_lo[i]))
            xf = _moe(xf, g2, route_i, route_w, counts, moe_gu[i], moe_dn[i])
    return xf.reshape(batch, seq_len, d)
```

```python
import functools

import numpy as np
import jax
import jax.numpy as jnp
from jax import lax
from jax.experimental import pallas as pl
from jax.experimental.pallas import tpu as pltpu

D_MODEL = 1024
HEAD_DIM = 64
GRID_W = 64
EPS = 1e-6
ROPE_THETA = 10000.0
NEG_INF = -1e30
LOG2_E = 1.4426950408889634
NA_HEADS = 4
NA_WIN_ROWS = 8
NA_WIN_COLS = 16
GQA_Q_HEADS = 6
GQA_KV_HEADS = 2
DILATIONS = (1, 4, 16)
DIL_SIDE = 64
QA_W, QB_W, QC_W = 256, 384, 384
KA_W, KB_W, KC_W = 256, 128, 384
Q_W = QA_W + QB_W + QC_W
KV_W = KA_W + KB_W + KC_W
QK_W = Q_W + KV_W
QKV_W = Q_W + 2 * KV_W
GATE_W = 3 * D_MODEL
N_EXPERTS = 8
LANES = 128
VMEM_LIMIT = 48 * 1024 * 1024
NA_UNROLL = 8
ROW_DMA_UNROLL = 8

GQA_HEAD_ORDER = (0, 3, 1, 4, 2, 5)

F32 = jnp.float32
BF16 = jnp.bfloat16


def _nt_dot(a, b):
    return lax.dot_general(a, b, (((1,), (1,)), ((), ())), preferred_element_type=F32)


def _rms(x, g):
    ms = jnp.mean(x * x, axis=-1, keepdims=True)
    return x * lax.rsqrt(ms + EPS) * g


_QK_CHUNKS = (
    (0, None, ("qa", 0)), (0, None, ("qa", 1)),
    (2, "axial", ("qb", 0)), (2, "axial", ("qb", 1)), (2, "axial", ("qb", 2)),
    (4, "seq", ("qc", 0)), (4, "seq", ("qc", 1)), (4, "seq", ("qc", 2)),
    (1, None, ("ka", 0)), (1, None, ("ka", 1)),
    (3, "axial", ("kb", 0)),
    (5, "seq", ("kc", 0)), (5, "seq", ("kc", 1)), (5, "seq", ("kc", 2)),
)


def _qkv_kernel(x_ref, g_ref, w_ref, gains_ref, bd_ref, cax_ref, sax_ref, cseq_ref, sseq_ref,
                qa_ref, ka_ref, va_ref, qb_ref, kb_ref, vb_ref, *c_refs):
    qc_refs, kc_refs, vc_refs = c_refs[0:3], c_refs[3:6], c_refs[6:9]
    tm = x_ref.shape[0]
    h = _rms(x_ref[...], g_ref[...]).astype(BF16)
    p = jnp.dot(h, w_ref[...], preferred_element_type=F32)
    lane = lax.broadcasted_iota(jnp.int32, (tm, LANES), 1)
    dests = {"qa": qa_ref, "ka": ka_ref, "qb": qb_ref, "kb": kb_ref, "qc": qc_refs, "kc": kc_refs}
    sumsq = None
    for c, (grow, rope, (dname, dj)) in enumerate(_QK_CHUNKS):
        if c % 2 == 0:
            pc2 = p[:, c * LANES:(c + 2) * LANES]
            sumsq = jnp.dot((pc2 * pc2).astype(BF16), bd_ref[...], preferred_element_type=F32)
        pc = p[:, c * LANES:(c + 1) * LANES]
        sc = sumsq[:, (c % 2) * LANES:(c % 2 + 1) * LANES]
        y = pc * lax.rsqrt(sc * (1.0 / HEAD_DIM) + EPS) * gains_ref[grow:grow + 1, :]
        if rope is not None:
            half = 16 if rope == "axial" else 32
            cos_t = cax_ref[...] if rope == "axial" else cseq_ref[...]
            sin_t = sax_ref[...] if rope == "axial" else sseq_ref[...]
            partner = jnp.where((lane % (2 * half)) < half,
                                pltpu.roll(y, LANES - half, axis=1), pltpu.roll(y, half, axis=1))
            y = y * cos_t + partner * sin_t
        yb = y.astype(BF16)
        d = dests[dname]
        if dname in ("qc", "kc"):
            d[dj][...] = yb
        else:
            d[:, dj * LANES:(dj + 1) * LANES] = yb
    v = p[:, QK_W:].astype(BF16)
    va_ref[...] = v[:, :KA_W]
    vb_ref[...] = v[:, KA_W:KA_W + KB_W]
    for j in range(3):
        vc_refs[j][...] = v[:, KA_W + KB_W + j * LANES:KA_W + KB_W + (j + 1) * LANES]


def _qkv_proj(x, g, w, gains, bd, tabs, seq_len, tm=512):
    n = x.shape[0]
    nb = seq_len // tm
    row = lambda i: (i, 0)
    const = lambda i: (0, 0)
    tab = lambda i: (i % nb, 0)
    widths = (QA_W, KA_W, KA_W, QB_W, KB_W, KB_W) + (LANES,) * 9
    out_shapes = [jax.ShapeDtypeStruct((n, w_), BF16) for w_ in widths]
    out_specs = [pl.BlockSpec((tm, w_), row) for w_ in widths]
    return pl.pallas_call(
        _qkv_kernel,
        grid=(n // tm,),
        in_specs=[pl.BlockSpec((tm, D_MODEL), row), pl.BlockSpec((1, D_MODEL), const),
                  pl.BlockSpec((D_MODEL, QKV_W), const), pl.BlockSpec((8, LANES), const),
                  pl.BlockSpec((2 * LANES, 2 * LANES), const)] + [pl.BlockSpec((tm, LANES), tab)] * 4,
        out_specs=out_specs,
        out_shape=out_shapes,
        compiler_params=pltpu.CompilerParams(dimension_semantics=("parallel",), vmem_limit_bytes=VMEM_LIMIT),
        name="qkv_proj",
    )(x, g, w, gains, bd, *tabs)


def _na_kernel(q_ref, k_ref, v_ref, bias_ref, o_ref, *, rows_per_step):
    rg = pl.program_id(1)
    n_rows = k_ref.shape[0] // GRID_W
    nk = NA_WIN_ROWS * GRID_W
    lo = lax.broadcasted_iota(jnp.int32, (GRID_W, LANES), 1) < HEAD_DIM

    def body(rr, carry):
        r = rg * rows_per_step + rr
        row_start = jnp.clip(r - NA_WIN_ROWS // 2, 0, n_rows - NA_WIN_ROWS)
        off = r - row_start
        ks = pl.multiple_of(row_start * GRID_W, GRID_W)
        qs = pl.multiple_of(rr * GRID_W, GRID_W)
        qrow = q_ref[pl.ds(qs, GRID_W), :]
        outs = []
        for p in range(NA_HEADS // 2):
            q2 = qrow[:, p * LANES:(p + 1) * LANES]
            z = jnp.zeros_like(q2)
            lhs = jnp.concatenate([jnp.where(lo, q2, z), jnp.where(lo, z, q2)], axis=0)
            kblk = k_ref[pl.ds(ks, nk), p * LANES:(p + 1) * LANES]
            vblk = v_ref[pl.ds(ks, nk), p * LANES:(p + 1) * LANES]
            s = _nt_dot(lhs, kblk)
            s = s + jnp.concatenate([bias_ref[2 * p, off], bias_ref[2 * p + 1, off]], axis=0)
            m = jnp.max(s, axis=-1, keepdims=True)
            e = jnp.exp(s - m)
            l = jnp.sum(e, axis=-1, keepdims=True)
            pv = jnp.dot(e.astype(BF16), vblk, preferred_element_type=F32) / l
            outs.append(jnp.where(lo, pv[:GRID_W], pv[GRID_W:]))
        o_ref[pl.ds(qs, GRID_W), :] = jnp.concatenate(outs, axis=1).astype(o_ref.dtype)
        return carry

    lax.fori_loop(0, rows_per_step, body, 0, unroll=NA_UNROLL)


def _na_bias_table(rpb):
    kw = NA_WIN_COLS
    c = np.arange(GRID_W)
    col_start = np.clip(c - kw // 2, 0, GRID_W - kw)
    col_mask = (c[None, :] >= col_start[:, None]) & (c[None, :] < col_start[:, None] + kw)
    dc = np.clip(c[None, :] - c[:, None], -(kw - 1), kw - 1) + NA_WIN_COLS - 1
    onehot = jnp.asarray(dc[None, :, :] == np.arange(2 * kw - 1)[:, None, None], F32)
    by_col = jnp.einsum("...rc,cqk->...rqk", rpb.astype(F32), onehot, precision=lax.Precision.HIGHEST)
    top = NA_WIN_ROWS - 1
    b = jnp.stack([by_col[..., top - off:top - off + NA_WIN_ROWS, :, :] for off in range(NA_WIN_ROWS)], axis=-4)
    b = jnp.swapaxes(b, -3, -2)
    b = jnp.where(col_mask[:, None, :], b, NEG_INF)
    return b.reshape(*rpb.shape[:-2], NA_WIN_ROWS, GRID_W, NA_WIN_ROWS * GRID_W)


def _na_attention(q, k, v, bias, batch, seq_len, rows_per_step=8):
    q3, k3, v3 = (t.reshape(batch, seq_len, QA_W) for t in (q, k, v))
    tq = rows_per_step * GRID_W
    out = pl.pallas_call(
        functools.partial(_na_kernel, rows_per_step=rows_per_step),
        grid=(batch, seq_len // tq),
        in_specs=[pl.BlockSpec((None, tq, QA_W), lambda b, i: (b, i, 0)),
                  pl.BlockSpec((None, seq_len, QA_W), lambda b, i: (b, 0, 0)),
                  pl.BlockSpec((None, seq_len, QA_W), lambda b, i: (b, 0, 0)),
                  pl.BlockSpec(bias.shape, lambda b, i: (0, 0, 0, 0))],
        out_specs=pl.BlockSpec((None, tq, QA_W), lambda b, i: (b, i, 0)),
        out_shape=jax.ShapeDtypeStruct((batch, seq_len, QA_W), BF16),
        compiler_params=pltpu.CompilerParams(dimension_semantics=("parallel", "parallel"),
                                             vmem_limit_bytes=VMEM_LIMIT),
        name="na_attn",
    )(q3, k3, v3, bias)
    return out.reshape(batch * seq_len, QA_W)


def _gqa_kernel(q_ref, k_ref, v_ref, o_ref, *, tk):
    tq = q_ref.shape[0]
    n_pairs = QB_W // LANES
    lo = lax.broadcasted_iota(jnp.int32, (tq, LANES), 1) < HEAD_DIM
    q = q_ref[...]
    z = jnp.zeros((tq, LANES), q.dtype)
    pairs = [q[:, p * LANES:(p + 1) * LANES] for p in range(n_pairs)]
    lhs = jnp.concatenate([jnp.where(lo, t, z) for t in pairs] + [jnp.where(lo, z, t) for t in pairs], axis=0)
    rows = lhs.shape[0]

    def body(kc, carry):
        m, l, acc = carry
        ks = pl.multiple_of(kc * tk, tk)
        kblk = k_ref[pl.ds(ks, tk), :]
        vblk = v_ref[pl.ds(ks, tk), :]
        s = _nt_dot(lhs, kblk)
        m_new = jnp.maximum(m, jnp.max(s, axis=-1, keepdims=True))
        e = jnp.exp2(s - m_new)
        alpha = jnp.exp2(m - m_new)
        l = alpha * l + jnp.sum(e, axis=-1, keepdims=True)
        acc = alpha * acc + jnp.dot(e.astype(BF16), vblk, preferred_element_type=F32)
        return m_new, l, acc

    init = (jnp.full((rows, 1), NEG_INF, F32), jnp.zeros((rows, 1), F32), jnp.zeros((rows, LANES), F32))
    _, l, acc = lax.fori_loop(0, k_ref.shape[0] // tk, body, init)
    o = acc / l
    half = n_pairs * tq
    for p in range(n_pairs):
        o_ref[:, p * LANES:(p + 1) * LANES] = jnp.where(
            lo, o[p * tq:(p + 1) * tq], o[half + p * tq:half + (p + 1) * tq]).astype(o_ref.dtype)


def _gqa_attention(q, k, v, batch, seq_len, tq=128, tk=4096):
    q3 = q.reshape(batch, seq_len, QB_W)
    k3 = k.reshape(batch, seq_len, KB_W)
    v3 = v.reshape(batch, seq_len, KB_W)
    out = pl.pallas_call(
        functools.partial(_gqa_kernel, tk=tk),
        grid=(batch, seq_len // tq),
        in_specs=[pl.BlockSpec((None, tq, QB_W), lambda b, i: (b, i, 0)),
                  pl.BlockSpec((None, seq_len, KB_W), lambda b, i: (b, 0, 0)),
                  pl.BlockSpec((None, seq_len, KB_W), lambda b, i: (b, 0, 0))],
        out_specs=pl.BlockSpec((None, tq, QB_W), lambda b, i: (b, i, 0)),
        out_shape=jax.ShapeDtypeStruct((batch, seq_len, QB_W), BF16),
        compiler_params=pltpu.CompilerParams(dimension_semantics=("parallel", "parallel"),
                                             vmem_limit_bytes=VMEM_LIMIT),
        name="gqa_attn",
    )(q3, k3, v3)
    return out.reshape(batch * seq_len, QB_W)


def _dil_kernel(q_ref, k_ref, v_ref, o_ref, lse_ref):
    sub_len = q_ref.shape[0]
    tq = 2 * DIL_SIDE
    tkk = 4 * DIL_SIDE
    lo = lax.broadcasted_iota(jnp.int32, (tq, LANES), 1) < HEAD_DIM
    qi_local = lax.broadcasted_iota(jnp.int32, (2 * tq, tkk), 0) % tq
    ki_local = lax.broadcasted_iota(jnp.int32, (2 * tq, tkk), 1)

    def body(qb, carry):
        i0 = pl.multiple_of(qb * tq, tq)
        ks = pl.multiple_of(jnp.clip(i0 - DIL_SIDE, 0, sub_len - tkk), DIL_SIDE)
        q2 = q_ref[pl.ds(i0, tq), :]
        z = jnp.zeros_like(q2)
        lhs = jnp.concatenate([jnp.where(lo, q2, z), jnp.where(lo, z, q2)], axis=0)
        s = _nt_dot(lhs, k_ref[pl.ds(ks, tkk), :])
        delta = (qi_local + i0) - (ki_local + ks)
        s = jnp.where((delta <= DIL_SIDE) & (delta >= -DIL_SIDE), s, NEG_INF)
        m = jnp.max(s, axis=-1, keepdims=True)
        e = jnp.exp(s - m)
        l = jnp.sum(e, axis=-1, keepdims=True)
        pv = jnp.dot(e.astype(BF16), v_ref[pl.ds(ks, tkk), :], preferred_element_type=F32) / l
        lse = m + jnp.log(l)
        o_ref[pl.ds(i0, tq), :] = jnp.where(lo, pv[:tq], pv[tq:])
        lse_ref[pl.ds(i0, tq), :] = jnp.where(lo, lse[:tq], lse[tq:])
        return carry

    lax.fori_loop(0, sub_len // tq, body, 0, unroll=min(4, sub_len // tq))


def _dil_attention(q, k, v, dil, batch, seq_len):
    sub_len = seq_len // dil
    view = lambda t: t.reshape(batch, sub_len, dil * LANES)
    spec = pl.BlockSpec((None, sub_len, LANES), lambda b, r: (b, 0, r))
    o, lse = pl.pallas_call(
        _dil_kernel,
        grid=(batch, dil),
        in_specs=[spec, spec, spec],
        out_specs=[spec, spec],
        out_shape=[jax.ShapeDtypeStruct((batch, sub_len, dil * LANES), F32)] * 2,
        compiler_params=pltpu.CompilerParams(dimension_semantics=("parallel", "parallel"),
                                             vmem_limit_bytes=VMEM_LIMIT),
        name=f"dil_attn_d{dil}",
    )(view(q), view(k), view(v))
    n = batch * seq_len
    return o.reshape(n, LANES), lse.reshape(n, LANES)


def _merge_body(x_ref, g_ref, oa_ref, ob_ref, oc0_ref, oc1_ref, oc2_ref, lse0_ref, lse1_ref, lse2_ref,
                wg_ref, wba_ref, wbb_ref, wbc_ref, wo_ref):
    x = x_ref[...]
    h = _rms(x, g_ref[...]).astype(BF16)
    lse = [lse0_ref[...], lse1_ref[...], lse2_ref[...]]
    mx = jnp.maximum(jnp.maximum(lse[0], lse[1]), lse[2])
    ex = [jnp.exp(t - mx) for t in lse]
    den = ex[0] + ex[1] + ex[2]
    oc = (ex[0] / den) * oc0_ref[...] + (ex[1] / den) * oc1_ref[...] + (ex[2] / den) * oc2_ref[...]
    ya = jnp.dot(oa_ref[...], wba_ref[...], preferred_element_type=F32)
    yb = jnp.dot(ob_ref[...], wbb_ref[...], preferred_element_type=F32)
    yc = jnp.dot(oc.astype(BF16), wbc_ref[...], preferred_element_type=F32)
    merged = None
    for j, y in enumerate((ya, yb, yc)):
        gate = jnp.dot(h, wg_ref[:, j * D_MODEL:(j + 1) * D_MODEL], preferred_element_type=F32)
        t = jax.nn.sigmoid(gate) * y
        merged = t if merged is None else merged + t
    return x + jnp.dot(merged.astype(BF16), wo_ref[...], preferred_element_type=F32)


def _merge_kernel(*refs):
    *ins, out_ref = refs
    out_ref[...] = _merge_body(*ins)


def _merge_route_kernel(*refs):
    *ins, g2_ref, wrh_ref, wrl_ref, tri_ref, out_ref, ri_ref, rw_ref, cnt_ref, carry_ref = refs
    xn = _merge_body(*ins)
    out_ref[...] = xn
    tm = xn.shape[0]

    @pl.when(pl.program_id(0) == 0)
    def _():
        carry_ref[...] = jnp.zeros_like(carry_ref)

    h2 = _rms(xn, g2_ref[...])
    h_hi = h2.astype(BF16)
    h_lo = (h2 - h_hi.astype(F32)).astype(BF16)
    logits = (jnp.dot(h_hi, wrh_ref[...], preferred_element_type=F32)
              + jnp.dot(h_lo, wrh_ref[...], preferred_element_type=F32)
              + jnp.dot(h_hi, wrl_ref[...], preferred_element_type=F32))
    lt = logits.T[:N_EXPERTS]
    row = lax.broadcasted_iota(jnp.int32, (N_EXPERTS, tm), 0)
    row_f = row.astype(F32)
    v1 = jnp.max(lt, axis=0, keepdims=True)
    i1 = jnp.min(jnp.where(lt == v1, row_f, float(N_EXPERTS)), axis=0, keepdims=True)
    hot1 = row_f == i1
    rest = jnp.where(hot1, -jnp.inf, lt)
    v2 = jnp.max(rest, axis=0, keepdims=True)
    i2 = jnp.min(jnp.where(rest == v2, row_f, float(N_EXPERTS)), axis=0, keepdims=True)
    hot2 = row_f == i2
    e2 = jnp.exp(v2 - v1)
    w1 = 1.0 / (1.0 + e2)
    w2 = e2 / (1.0 + e2)
    hot = jnp.where(hot1 | hot2, 1.0, 0.0)
    hot16 = jnp.concatenate([hot, jnp.zeros_like(hot)], axis=0).astype(BF16)
    carry = carry_ref[:, 0:1]
    before = jnp.dot(hot16, tri_ref[...], preferred_element_type=F32)[:N_EXPERTS] + carry
    r1 = jnp.sum(jnp.where(hot1, before, 0.0), axis=0, keepdims=True)
    r2 = jnp.sum(jnp.where(hot2, before, 0.0), axis=0, keepdims=True)
    total = carry + jnp.sum(hot, axis=1, keepdims=True)
    carry_ref[...] = jnp.broadcast_to(total, carry_ref.shape)
    cnt_ref[...] = jnp.broadcast_to(total, cnt_ref.shape)
    ri = jnp.where(row == 0, i1, jnp.where(row == 1, i2, jnp.where(row == 2, r1, r2)))
    ri_ref[...] = ri.astype(jnp.int32)
    rw_ref[...] = jnp.where(row == 0, w1, jnp.where(row == 1, w2, 0.0))


def _merge(x, g1, oa, ob, oc, lse, wg, wba, wbb, wbc, wo, router=None, tm=512):
    n = x.shape[0]
    row = lambda i: (i, 0)
    const = lambda i: (0, 0)
    in_specs = [pl.BlockSpec((tm, D_MODEL), row), pl.BlockSpec((1, D_MODEL), const),
                pl.BlockSpec((tm, QA_W), row), pl.BlockSpec((tm, QB_W), row)]
    in_specs += [pl.BlockSpec((tm, LANES), row)] * 6
    in_specs += [pl.BlockSpec((D_MODEL, GATE_W), const), pl.BlockSpec((QA_W, D_MODEL), const),
                 pl.BlockSpec((QB_W, D_MODEL), const), pl.BlockSpec((LANES, D_MODEL), const),
                 pl.BlockSpec((D_MODEL, D_MODEL), const)]
    args = [x, g1, oa, ob, *oc, *lse, wg, wba, wbb, wbc, wo]
    x_shape = jax.ShapeDtypeStruct((n, D_MODEL), F32)
    x_spec = pl.BlockSpec((tm, D_MODEL), row)
    if router is None:
        return pl.pallas_call(
            _merge_kernel, grid=(n // tm,), in_specs=in_specs, out_specs=x_spec, out_shape=x_shape,
            compiler_params=pltpu.CompilerParams(dimension_semantics=("parallel",), vmem_limit_bytes=VMEM_LIMIT),
            name="merge",
        )(*args)
    g2, wr_hi, wr_lo = router
    tri = jnp.triu(jnp.ones((tm, tm), BF16), k=1)
    col = lambda i: (0, i)
    in_specs += [pl.BlockSpec((1, D_MODEL), const), pl.BlockSpec((D_MODEL, LANES), const),
                 pl.BlockSpec((D_MODEL, LANES), const), pl.BlockSpec((tm, tm), const)]
    return pl.pallas_call(
        _merge_route_kernel, grid=(n // tm,), in_specs=in_specs,
        out_specs=[x_spec, pl.BlockSpec((N_EXPERTS, tm), col), pl.BlockSpec((N_EXPERTS, tm), col),
                   pl.BlockSpec((N_EXPERTS, LANES), const)],
        out_shape=[x_shape, jax.ShapeDtypeStruct((N_EXPERTS, n), jnp.int32),
                   jax.ShapeDtypeStruct((N_EXPERTS, n), F32), jax.ShapeDtypeStruct((N_EXPERTS, LANES), F32)],
        scratch_shapes=[pltpu.VMEM((N_EXPERTS, LANES), F32)],
        compiler_params=pltpu.CompilerParams(dimension_semantics=("arbitrary",), vmem_limit_bytes=VMEM_LIMIT),
        name="merge_route",
    )(*args, g2, wr_hi, wr_lo, tri)


def _ffn_kernel(te_ref, tv_ref, x_ref, g_ref, wg_ref, wu_ref, wd_ref, o_ref, h_ref, acc_ref, *, residual):
    i = pl.program_id(0)
    f = pl.program_id(1)

    @pl.when(f == 0)
    def _():
        h_ref[...] = _rms(x_ref[...], g_ref[...]).astype(BF16)
        acc_ref[...] = jnp.zeros_like(acc_ref)

    @pl.when(tv_ref[i] == 1)
    def _():
        h = h_ref[...]
        gg = jnp.dot(h, wg_ref[...].astype(BF16), preferred_element_type=F32)
        uu = jnp.dot(h, wu_ref[...].astype(BF16), preferred_element_type=F32)
        a = (jax.nn.silu(gg) * uu).astype(BF16)
        acc_ref[...] += jnp.dot(a, wd_ref[...].astype(BF16), preferred_element_type=F32)

    @pl.when(f == pl.num_programs(1) - 1)
    def _():
        o_ref[...] = (x_ref[...] + acc_ref[...]) if residual else acc_ref[...]


def _ffn(x, g, w_gu, w_down, tile_expert, tile_valid, *, residual, tm, tf):
    rows = x.shape[0]
    f_dim = w_down.shape[1]
    nf = f_dim // tf
    fe = lambda f, tv, i: f * tv[i] + (nf - 1) * (1 - tv[i])
    grid_spec = pltpu.PrefetchScalarGridSpec(
        num_scalar_prefetch=2,
        grid=(rows // tm, nf),
        in_specs=[pl.BlockSpec((tm, D_MODEL), lambda i, f, te, tv: (i, 0)),
                  pl.BlockSpec((1, D_MODEL), lambda i, f, te, tv: (0, 0)),
                  pl.BlockSpec((None, D_MODEL, tf), lambda i, f, te, tv: (te[i], 0, fe(f, tv, i))),
                  pl.BlockSpec((None, D_MODEL, tf), lambda i, f, te, tv: (te[i], 0, nf + fe(f, tv, i))),
                  pl.BlockSpec((None, tf, D_MODEL), lambda i, f, te, tv: (te[i], fe(f, tv, i), 0))],
        out_specs=pl.BlockSpec((tm, D_MODEL), lambda i, f, te, tv: (i, 0)),
        scratch_shapes=[pltpu.VMEM((tm, D_MODEL), BF16), pltpu.VMEM((tm, D_MODEL), F32)],
    )
    return pl.pallas_call(
        functools.partial(_ffn_kernel, residual=residual),
        grid_spec=grid_spec,
        out_shape=jax.ShapeDtypeStruct((rows, D_MODEL), F32),
        compiler_params=pltpu.CompilerParams(dimension_semantics=("parallel", "arbitrary"),
                                             vmem_limit_bytes=VMEM_LIMIT),
        name="ffn_res" if residual else "ffn_moe",
    )(tile_expert, tile_valid, x, g, w_gu, w_gu, w_down)


def _dispatch_kernel(pos_ref, x_ref, zeros_ref, xs_ref, sem):
    del zeros_ref
    td = x_ref.shape[0]

    def row_copy(t, slot):
        return pltpu.make_async_copy(x_ref.at[pl.ds(t, 1)], xs_ref.at[pl.ds(pos_ref[0, 2 * t + slot], 1)], sem)

    def start(t, carry):
        row_copy(t, 0).start()
        row_copy(t, 1).start()
        return carry

    def wait(t, carry):
        row_copy(t, 0).wait()
        row_copy(t, 1).wait()
        return carry

    lax.fori_loop(0, td, start, 0, unroll=ROW_DMA_UNROLL)
    lax.fori_loop(0, td, wait, 0, unroll=ROW_DMA_UNROLL)


def _dispatch(x, pos, rows_out, td=256):
    n = x.shape[0]
    pos3 = pos.reshape(n // td, 1, 2 * td)
    zeros = jnp.zeros((rows_out, D_MODEL), F32)
    return pl.pallas_call(
        _dispatch_kernel,
        grid=(n // td,),
        in_specs=[pl.BlockSpec((None, 1, 2 * td), lambda i: (i, 0, 0), memory_space=pltpu.SMEM),
                  pl.BlockSpec((td, D_MODEL), lambda i: (i, 0)),
                  pl.BlockSpec(memory_space=pl.ANY)],
        out_specs=pl.BlockSpec(memory_space=pl.ANY),
        out_shape=jax.ShapeDtypeStruct((rows_out, D_MODEL), F32),
        scratch_shapes=[pltpu.SemaphoreType.DMA(())],
        input_output_aliases={2: 0},
        compiler_params=pltpu.CompilerParams(dimension_semantics=("arbitrary",), vmem_limit_bytes=VMEM_LIMIT),
        name="moe_dispatch",
    )(pos3, x, zeros)


def _combine_kernel(pos_ref, x_ref, w_ref, ys_ref, o_ref, buf_ref, sem):
    tc = x_ref.shape[0]

    def row_copy(t, slot):
        return pltpu.make_async_copy(ys_ref.at[pl.ds(pos_ref[0, 2 * t + slot], 1)],
                                     buf_ref.at[slot, pl.ds(t, 1)], sem)

    def start(t, carry):
        row_copy(t, 0).start()
        row_copy(t, 1).start()
        return carry

    def wait(t, carry):
        row_copy(t, 0).wait()
        row_copy(t, 1).wait()
        return carry

    lax.fori_loop(0, tc, start, 0, unroll=ROW_DMA_UNROLL)
    lax.fori_loop(0, tc, wait, 0, unroll=ROW_DMA_UNROLL)
    w = w_ref[...]
    o_ref[...] = x_ref[...] + (w[:, 0:1] * buf_ref[0] + w[:, 1:2] * buf_ref[1])


def _combine(x, w, ys, pos, tc=256):
    n = x.shape[0]
    pos3 = pos.reshape(n // tc, 1, 2 * tc)
    return pl.pallas_call(
        _combine_kernel,
        grid=(n // tc,),
        in_specs=[pl.BlockSpec((None, 1, 2 * tc), lambda i: (i, 0, 0), memory_space=pltpu.SMEM),
                  pl.BlockSpec((tc, D_MODEL), lambda i: (i, 0)),
                  pl.BlockSpec((tc, LANES), lambda i: (i, 0)),
                  pl.BlockSpec(memory_space=pl.ANY)],
        out_specs=pl.BlockSpec((tc, D_MODEL), lambda i: (i, 0)),
        out_shape=jax.ShapeDtypeStruct((n, D_MODEL), F32),
        scratch_shapes=[pltpu.VMEM((2, tc, D_MODEL), F32), pltpu.SemaphoreType.DMA(())],
        compiler_params=pltpu.CompilerParams(dimension_semantics=("arbitrary",), vmem_limit_bytes=VMEM_LIMIT),
        name="moe_combine",
    )(pos3, x, w, ys)


def _moe(x, g2, route_i, route_w, counts, w_gu, w_down, expert_base, tm=1024, tf=512):
    n = x.shape[0]
    n_tiles = (2 * n) // tm + N_EXPERTS
    cnt = counts[:, 0].astype(jnp.int32)
    padded = ((cnt + tm - 1) // tm) * tm
    off_end = jnp.cumsum(padded)
    off_start = off_end - padded
    experts = route_i[0:2].T
    pos = off_start[experts] + route_i[2:4].T
    weights = jnp.pad(route_w[0:2].T, ((0, 0), (0, LANES - 2)))
    tile_start = jnp.arange(n_tiles, dtype=jnp.int32) * tm
    tile_valid = (tile_start < off_end[-1]).astype(jnp.int32)
    tile_expert = jnp.minimum(jnp.sum(tile_start[:, None] >= off_end[None, :], axis=1), N_EXPERTS - 1)
    last_expert = tile_expert[jnp.maximum(off_end[-1] // tm - 1, 0)]
    tile_expert = (jnp.where(tile_valid == 1, tile_expert, last_expert) + expert_base).astype(jnp.int32)
    xs = _dispatch(x, pos, n_tiles * tm)
    ys = _ffn(xs, g2, w_gu, w_down, tile_expert, tile_valid, residual=False, tm=tm, tf=tf)
    return _combine(x, weights, ys, pos)


def _rope_tables(seq_len):
    def angles(pos, dim):
        inv_freq = ROPE_THETA ** (-jnp.arange(0, dim, 2, dtype=F32) / dim)
        return pos.astype(F32)[:, None] * inv_freq[None, :]

    pos = jnp.arange(seq_len)
    a_seq = angles(pos, HEAD_DIM)
    a_row = angles(pos // GRID_W, HEAD_DIM // 2)
    a_col = angles(pos % GRID_W, HEAD_DIM // 2)
    cos_seq = jnp.concatenate([jnp.cos(a_seq)] * 2, axis=-1)
    sin_seq = jnp.concatenate([-jnp.sin(a_seq), jnp.sin(a_seq)], axis=-1)
    cos_ax = jnp.concatenate([jnp.cos(a_row)] * 2 + [jnp.cos(a_col)] * 2, axis=-1)
    sin_ax = jnp.concatenate([-jnp.sin(a_row), jnp.sin(a_row), -jnp.sin(a_col), jnp.sin(a_col)], axis=-1)
    two = lambda t: jnp.concatenate([t, t], axis=-1)
    return two(cos_ax), two(sin_ax), two(cos_seq), two(sin_seq)


def _qkv_column_order():
    cols = np.arange(QKV_W)
    qb = QA_W + np.concatenate([np.arange(h * HEAD_DIM, (h + 1) * HEAD_DIM) for h in GQA_HEAD_ORDER])
    cols[QA_W:QA_W + QB_W] = qb
    return cols


def _branch_row_order():
    rows = np.arange(QA_W + QB_W + LANES)
    rows[QA_W:QA_W + QB_W] = QA_W + np.concatenate(
        [np.arange(h * HEAD_DIM, (h + 1) * HEAD_DIM) for h in GQA_HEAD_ORDER])
    return rows


def kernel(x, norm_gains, w_in, qk_gains, na_rpb, w_branch, w_out, ffn_w_gu, ffn_w_down,
           moe_router, moe_w_gu, moe_w_down):
    batch, seq_len, d = x.shape
    depth = w_in.shape[0]
    n = batch * seq_len
    assert d == D_MODEL and seq_len == GRID_W * GRID_W

    tabs = _rope_tables(seq_len)
    na_bias = _na_bias_table(na_rpb)
    w_qkv = w_in[:, :, _qkv_column_order()].astype(BF16)
    w_gate = w_in[:, :, QKV_W:].astype(BF16)
    w_br = w_branch[:, _branch_row_order(), :].astype(BF16)
    w_o = w_out.astype(BF16)
    ffn_gu = ffn_w_gu.astype(BF16)
    ffn_dn = ffn_w_down.astype(BF16)
    moe_gu = moe_w_gu.reshape((-1,) + moe_w_gu.shape[2:])
    moe_dn = moe_w_down.reshape((-1,) + moe_w_down.shape[2:])
    router = jnp.pad(moe_router, ((0, 0), (0, 0), (0, LANES - N_EXPERTS)))
    router_hi = router.astype(BF16)
    router_lo = (router - router_hi.astype(F32)).astype(BF16)
    gains = jnp.tile(qk_gains.reshape(depth, 6, HEAD_DIM), (1, 1, 2))
    q_scale = HEAD_DIM ** -0.5
    gains = gains * jnp.asarray([q_scale, 1.0, q_scale * LOG2_E, 1.0, q_scale, 1.0], F32)[None, :, None]
    gains = jnp.pad(gains, ((0, 0), (0, 2), (0, 0)))
    hd_idx = np.arange(2 * LANES) // HEAD_DIM
    block_ones = jnp.asarray(hd_idx[:, None] == hd_idx[None, :], BF16)
    one_tile = jnp.zeros((n // 1024,), jnp.int32)
    all_valid = jnp.ones((n // 1024,), jnp.int32)

    xf = x.reshape(n, d)
    for layer in range(depth):
        g1 = norm_gains[layer, 0][None, :]
        g2 = norm_gains[layer, 1][None, :]
        qa, ka, va, qb, kb, vb, *cs = _qkv_proj(xf, g1, w_qkv[layer], gains[layer], block_ones, tabs, seq_len)
        qc, kc, vc = cs[0:3], cs[3:6], cs[6:9]
        oa = _na_attention(qa, ka, va, na_bias[layer], batch, seq_len)
        ob = _gqa_attention(qb, kb, vb, batch, seq_len)
        dil = [_dil_attention(qc[j], kc[j], vc[j], DILATIONS[j], batch, seq_len) for j in range(3)]
        oc = [t[0] for t in dil]
        lse = [t[1] for t in dil]
        wb = w_br[layer]
        margs = (xf, g1, oa, ob, oc, lse, w_gate[layer], wb[:QA_W], wb[QA_W:QA_W + QB_W], wb[QA_W + QB_W:], w_o[layer])
        if layer % 2 == 0:
            xf = _merge(*margs)
            xf = _ffn(xf, g2, ffn_gu[layer // 2][None], ffn_dn[layer // 2][None], one_tile, all_valid,
                      residual=True, tm=1024, tf=256)
        else:
            i = layer // 2
            xf, route_i, route_w, counts = _merge(*margs, router=(g2, router_hi[i], router_lo[i]))
            xf = _moe(xf, g2, route_i, route_w, counts, moe_gu, moe_dn, i * N_EXPERTS)
    return xf.reshape(batch, seq_len, d)
```
